```python
import jax, jax.numpy as jnp
from jax import lax
import numpy as np

D_MODEL = 4096
BATCH = 2
SEQ = 8192
DEPTH = 1

MIX_WIDTH = D_MODEL
DN_WIDTH = MIX_WIDTH // 2
DN_HEAD_DIM = 128
DN_HEADS = DN_WIDTH // DN_HEAD_DIM
DN_CONV = 4
DN_CHUNK = 64
POOL_WIDTH = MIX_WIDTH - DN_WIDTH
POOL_WINDOWS = (2, 4, 8, 16)
POOL_GROUPS = len(POOL_WINDOWS)
POOL_GROUP_WIDTH = POOL_WIDTH // POOL_GROUPS
D_FF = 4 * D_MODEL
N_MOD = 6
IN_COLS = 4 * DN_WIDTH + 2 * DN_HEADS + POOL_WIDTH
EPS = 1e-6

kernel_name = "hybrid_deltanet_pool_adaln_block"


def rms_norm(x, g):
    xf = x.astype(jnp.float32)
    y = xf * lax.rsqrt(jnp.mean(xf * xf, axis=-1, keepdims=True) + EPS)
    return (y * g.astype(jnp.float32)).astype(x.dtype)


def l2_normalize(x):
    return x * lax.rsqrt(jnp.sum(x * x, axis=-1, keepdims=True) + EPS)


def causal_depthwise_conv(u, w):
    K = w.shape[0]
    up = jnp.pad(u, ((0, 0), (K - 1, 0), (0, 0)))
    return lax.conv_general_dilated(
        up, w.astype(u.dtype)[:, None, :], window_strides=(1,), padding='VALID',
        dimension_numbers=('NWC', 'WIO', 'NWC'), feature_group_count=u.shape[-1])


def chunk_gated_delta_rule(q, k, v, g, beta):
    B, S, H, DK = q.shape
    DV = v.shape[-1]
    C = DN_CHUNK
    N = S // C
    q = l2_normalize(q) * (DK ** -0.5)
    k = l2_normalize(k)

    def to_chunks(t):
        return t.reshape(B, N, C, H, t.shape[-1]).transpose(0, 3, 1, 2, 4)

    q, k, v = to_chunks(q), to_chunks(k), to_chunks(v)
    g = g.reshape(B, N, C, H).transpose(0, 3, 1, 2)
    beta = beta.reshape(B, N, C, H).transpose(0, 3, 1, 2)
    g = jnp.cumsum(g, axis=-1)

    causal = jnp.tril(jnp.ones((C, C), dtype=bool))
    strict = jnp.tril(jnp.ones((C, C), dtype=bool), -1)
    diff = g[..., :, None] - g[..., None, :]
    decay = jnp.where(causal, jnp.exp(jnp.where(causal, diff, 0.0)), 0.0)

    k_beta = k * beta[..., None]
    v_beta = v * beta[..., None]
    kk = jnp.einsum('bhnid,bhnjd->bhnij', k_beta, k)
    a_mat = jnp.eye(C, dtype=q.dtype) + jnp.where(strict, kk * decay, 0.0)
    rhs = jnp.concatenate([v_beta, k_beta * jnp.exp(g)[..., None]], axis=-1)
    sol = lax.linalg.triangular_solve(a_mat, rhs, left_side=True, lower=True, unit_diagonal=True)
    u_new = sol[..., :DV]
    w_cum = sol[..., DV:]

    qk = jnp.where(causal, jnp.einsum('bhnid,bhnjd->bhnij', q, k) * decay, 0.0)
    g_last = g[..., -1]
    k_dec = k * jnp.exp(g_last[..., None] - g)[..., None]
    q_dec = q * jnp.exp(g)[..., None]

    xs = tuple(jnp.moveaxis(t, 2, 0) for t in (q_dec, qk, u_new, w_cum, k_dec, g_last))

    def step(state, inp):
        q_i, qk_i, u_i, w_i, kd_i, gl_i = inp
        v_i = u_i - jnp.einsum('bhck,bhkv->bhcv', w_i, state)
        o_i = jnp.einsum('bhck,bhkv->bhcv', q_i, state) + jnp.einsum('bhij,bhjv->bhiv', qk_i, v_i)
        state = state * jnp.exp(gl_i)[..., None, None] + jnp.einsum('bhck,bhcv->bhkv', kd_i, v_i)
        return state, o_i

    s0 = jnp.zeros((B, H, DK, DV), dtype=q.dtype)
    _, o = lax.scan(step, s0, xs)
    return o.transpose(1, 0, 3, 2, 4).reshape(B, S, H, DV)


def multiscale_pool(u, pool_w, pool_scale):
    B, S, _ = u.shape
    ug = u.astype(jnp.float32).reshape(B, S, POOL_GROUPS, POOL_GROUP_WIDTH)
    cs = jnp.pad(jnp.cumsum(ug, axis=1), ((0, 0), (1, 0), (0, 0), (0, 0)))
    t = jnp.arange(S)[:, None]
    win = jnp.asarray(POOL_WINDOWS, dtype=jnp.int32)[None, :]
    start = jnp.maximum(t + 1 - win, 0)
    grp = jnp.arange(POOL_GROUPS)[None, :]
    window_sum = cs[:, 1:] - cs[:, start, grp]
    count = (t + 1 - start).astype(jnp.float32)
    pooled = window_sum / count[None, :, :, None] - ug
    p = jnp.einsum('bsgc,gcd->bsgd', pooled, pool_w.astype(jnp.float32))
    p = p * lax.rsqrt(jnp.mean(p * p, axis=-1, keepdims=True) + EPS)
    return p.reshape(B, S, POOL_WIDTH) * pool_scale.astype(jnp.float32)


def hybrid_layer(x, c, w_ada, b_ada, norm1_g, w_in, conv_w, a_log, dt_bias, dn_norm_g,
                 pool_w, pool_scale, w_out, norm2_g, w_ff1, w_ff2):
    B, S, _ = x.shape
    f32 = jnp.float32
    mod = jax.nn.silu(c) @ w_ada + b_ada
    sh1, sc1, gt1, sh2, sc2, gt2 = [m[:, None, :] for m in jnp.split(mod, N_MOD, axis=-1)]

    h = rms_norm(x, norm1_g) * (1 + sc1) + sh1
    proj = h @ w_in
    o1 = 3 * DN_WIDTH
    o2 = 4 * DN_WIDTH
    o3 = o2 + DN_HEADS
    o4 = o3 + DN_HEADS
    qkv, z, b_logit, a_logit, pool_in = (proj[..., :o1], proj[..., o1:o2], proj[..., o2:o3],
                                         proj[..., o3:o4], proj[..., o4:])

    qkv = jax.nn.silu(causal_depthwise_conv(qkv, conv_w))
    q, k, v = [t.astype(f32).reshape(B, S, DN_HEADS, DN_HEAD_DIM) for t in jnp.split(qkv, 3, axis=-1)]
    beta = jax.nn.sigmoid(b_logit.astype(f32))
    g = -jnp.exp(a_log.astype(f32)) * jax.nn.softplus(a_logit.astype(f32) + dt_bias.astype(f32))
    o = chunk_gated_delta_rule(q, k, v, g, beta)
    o = rms_norm(o, dn_norm_g) * jax.nn.silu(z.astype(f32).reshape(B, S, DN_HEADS, DN_HEAD_DIM))
    o = o.reshape(B, S, DN_WIDTH).astype(x.dtype)

    p = multiscale_pool(pool_in, pool_w, pool_scale).astype(x.dtype)

    y = jnp.concatenate([o, p], axis=-1) @ w_out
    x = x + gt1 * y

    h = rms_norm(x, norm2_g) * (1 + sc2) + sh2
    m = jnp.square(jax.nn.relu(h @ w_ff1)) @ w_ff2
    return x + gt2 * m


def setup_inputs(seed: int = 0) -> dict:
    key = jax.random.key(seed)
    ks = jax.random.split(key, 20)
    f32 = jnp.float32
    L = DEPTH

    def normal(k, shape, scale):
        return jax.random.normal(k, shape, f32) * scale

    dt = jnp.exp(jax.random.uniform(ks[7], (L, DN_HEADS), f32, np.log(1e-3), np.log(1e-1)))
    return {
        'x': normal(ks[0], (BATCH, SEQ, D_MODEL), 1.0),
        'c': normal(ks[1], (BATCH, D_MODEL), 1.0),
        'w_ada': normal(ks[2], (L, D_MODEL, N_MOD * D_MODEL), D_MODEL ** -0.5),
        'b_ada': normal(ks[3], (L, N_MOD * D_MODEL), 0.02),
        'norm1_g': 1.0 + normal(ks[4], (L, D_MODEL), 0.05),
        'w_in': normal(ks[5], (L, D_MODEL, IN_COLS), D_MODEL ** -0.5),
        'conv_w': normal(ks[6], (L, DN_CONV, 3 * DN_WIDTH), DN_CONV ** -0.5),
        'a_log': jnp.log(jax.random.uniform(ks[8], (L, DN_HEADS), f32, 1.0, 16.0)),
        'dt_bias': dt + jnp.log(-jnp.expm1(-dt)),
        'dn_norm_g': 1.0 + normal(ks[9], (L, DN_HEAD_DIM), 0.05),
        'pool_w': normal(ks[10], (L, POOL_GROUPS, POOL_GROUP_WIDTH, POOL_GROUP_WIDTH), POOL_GROUP_WIDTH ** -0.5),
        'pool_scale': 1.0 + normal(ks[11], (L, POOL_WIDTH), 0.1),
        'w_out': normal(ks[12], (L, MIX_WIDTH, D_MODEL), MIX_WIDTH ** -0.5),
        'norm2_g': 1.0 + normal(ks[13], (L, D_MODEL), 0.05),
        'w_ff1': normal(ks[14], (L, D_MODEL, D_FF), D_MODEL ** -0.5),
        'w_ff2': normal(ks[15], (L, D_FF, D_MODEL), D_FF ** -0.5),
        'final_norm_g': 1.0 + normal(ks[16], (D_MODEL,), 0.05),
    }


def reference(x, c, w_ada, b_ada, norm1_g, w_in, conv_w, a_log, dt_bias, dn_norm_g,
              pool_w, pool_scale, w_out, norm2_g, w_ff1, w_ff2, final_norm_g):
    for l in range(DEPTH):
        x = hybrid_layer(x, c, w_ada[l], b_ada[l], norm1_g[l], w_in[l], conv_w[l], a_log[l],
                         dt_bias[l], dn_norm_g[l], pool_w[l], pool_scale[l], w_out[l],
                         norm2_g[l], w_ff1[l], w_ff2[l])
    return rms_norm(x, final_norm_g)
```

```python
import functools

import jax
import jax.numpy as jnp
from jax import lax
from jax.experimental import pallas as pl
from jax.experimental.pallas import tpu as pltpu

F32 = jnp.float32
BF16 = jnp.bfloat16
EPS = 1e-6

V7X_LANES = 128
V7X_SUBLANES = 8
V7X_VMEM_LIMIT_BYTES = 56 * 1024 * 1024

POOL_WINDOWS = (2, 4, 8, 16)
DELTA_CHUNK = 128
SOLVE_BASE = 16
N_MOD = 6
N_GATE_QUANTITIES = 5


def _params(semantics):
    return pltpu.CompilerParams(dimension_semantics=semantics, vmem_limit_bytes=V7X_VMEM_LIMIT_BYTES)


def _silu(x):
    return x * jax.nn.sigmoid(x)


def _dot(a, b):
    return jnp.dot(a, b, preferred_element_type=F32)


def _dot_nt(a, b):
    return lax.dot_general(a, b, (((1,), (1,)), ((), ())), preferred_element_type=F32)


def _dot_tn(a, b):
    return lax.dot_general(a, b, (((0,), (0,)), ((), ())), preferred_element_type=F32)


def _mod_kernel(c_ref, w_ref, b_ref, o_ref):
    a = _silu(c_ref[...]).astype(BF16)
    o_ref[...] = _dot(a, w_ref[...].astype(BF16)) + b_ref[...]


def _modulation(c, w_ada, b_ada):
    bsz, d = c.shape
    n = w_ada.shape[1]
    rows = V7X_SUBLANES
    c_pad = jnp.zeros((rows, d), F32).at[:bsz].set(c)
    bn = min(512, n)
    out = pl.pallas_call(
        _mod_kernel,
        grid=(n // bn,),
        in_specs=[
            pl.BlockSpec((rows, d), lambda j: (0, 0)),
            pl.BlockSpec((d, bn), lambda j: (0, j)),
            pl.BlockSpec((1, bn), lambda j: (0, j)),
        ],
        out_specs=pl.BlockSpec((rows, bn), lambda j: (0, j)),
        out_shape=jax.ShapeDtypeStruct((rows, n), F32),
        compiler_params=_params(("parallel",)),
        name="adaln_modulation",
    )(c_pad, w_ada, b_ada.reshape(1, n))
    return out[:bsz]


def _norm_mod_kernel(x_ref, g_ref, sc_ref, sh_ref, o_ref):
    x = x_ref[...]
    y = x * lax.rsqrt(jnp.mean(x * x, axis=-1, keepdims=True) + EPS)
    y = y * g_ref[...]
    o_ref[...] = (y * (1.0 + sc_ref[0]) + sh_ref[0]).astype(o_ref.dtype)


def _norm_kernel(x_ref, g_ref, o_ref):
    x = x_ref[...]
    y = x * lax.rsqrt(jnp.mean(x * x, axis=-1, keepdims=True) + EPS)
    o_ref[...] = (y * g_ref[...]).astype(o_ref.dtype)


def _norm_mod(x2, g, sc, sh, seq):
    t, d = x2.shape
    bt = min(256, seq)
    per_seq = seq // bt
    return pl.pallas_call(
        _norm_mod_kernel,
        grid=(t // bt,),
        in_specs=[
            pl.BlockSpec((bt, d), lambda i: (i, 0)),
            pl.BlockSpec((1, d), lambda i: (0, 0)),
            pl.BlockSpec((1, 1, d), lambda i: (i // per_seq, 0, 0)),
            pl.BlockSpec((1, 1, d), lambda i: (i // per_seq, 0, 0)),
        ],
        out_specs=pl.BlockSpec((bt, d), lambda i: (i, 0)),
        out_shape=jax.ShapeDtypeStruct((t, d), BF16),
        compiler_params=_params(("parallel",)),
        name="rmsnorm_modulate",
    )(x2, g.reshape(1, d), sc, sh)


def _final_norm(x2, g):
    t, d = x2.shape
    bt = min(256, t)
    return pl.pallas_call(
        _norm_kernel,
        grid=(t // bt,),
        in_specs=[pl.BlockSpec((bt, d), lambda i: (i, 0)), pl.BlockSpec((1, d), lambda i: (0, 0))],
        out_specs=pl.BlockSpec((bt, d), lambda i: (i, 0)),
        out_shape=jax.ShapeDtypeStruct((t, d), F32),
        compiler_params=_params(("parallel",)),
        name="final_rmsnorm",
    )(x2, g.reshape(1, d))


def _mm_kernel(a_ref, w_ref, o_ref):
    o_ref[...] = _dot(a_ref[...], w_ref[...]).astype(o_ref.dtype)


def _mm_relu2_kernel(a_ref, w_ref, o_ref):
    r = jnp.maximum(_dot(a_ref[...], w_ref[...]), 0.0)
    o_ref[...] = (r * r).astype(o_ref.dtype)


def _fit(block, dim):
    block = min(block, dim)
    while dim % block:
        block -= V7X_LANES
    return block


def _matmul(a, w, out_dtype, kernel, bm, bn, name):
    m, k = a.shape
    n = w.shape[1]
    bm, bn = _fit(bm, m), _fit(bn, n)
    return pl.pallas_call(
        kernel,
        grid=(m // bm, n // bn),
        in_specs=[pl.BlockSpec((bm, k), lambda i, j: (i, 0)), pl.BlockSpec((k, bn), lambda i, j: (0, j))],
        out_specs=pl.BlockSpec((bm, bn), lambda i, j: (i, j)),
        out_shape=jax.ShapeDtypeStruct((m, n), out_dtype),
        compiler_params=_params(("parallel", "parallel")),
        name=name,
    )(a, w)


def _mm_out_kernel(o_ref, p_ref, wo_ref, wp_ref, x_ref, gt_ref, out_ref):
    acc = _dot(o_ref[...], wo_ref[...]) + _dot(p_ref[...], wp_ref[...])
    out_ref[...] = x_ref[...] + gt_ref[0] * acc


def _out_proj_residual(o, p, w_out, x2, gt, seq, bm=1024, bn=1024):
    t, ko = o.shape
    kp = p.shape[1]
    d = w_out.shape[1]
    bm, bn = min(bm, seq), min(bn, d)
    per_seq = seq // bm
    assert ko == kp
    return pl.pallas_call(
        _mm_out_kernel,
        grid=(t // bm, d // bn),
        in_specs=[
            pl.BlockSpec((bm, ko), lambda i, j: (i, 0)),
            pl.BlockSpec((bm, kp), lambda i, j: (i, 0)),
            pl.BlockSpec((ko, bn), lambda i, j: (0, j)),
            pl.BlockSpec((kp, bn), lambda i, j: (1, j)),
            pl.BlockSpec((bm, bn), lambda i, j: (i, j)),
            pl.BlockSpec((1, 1, bn), lambda i, j: (i // per_seq, 0, j)),
        ],
        out_specs=pl.BlockSpec((bm, bn), lambda i, j: (i, j)),
        out_shape=jax.ShapeDtypeStruct((t, d), F32),
        compiler_params=_params(("parallel", "parallel")),
        name="out_proj_residual",
    )(o, p, w_out, w_out, x2, gt)


def _mm_down_kernel(a_ref, w_ref, x_ref, gt_ref, o_ref, acc_ref, *, nk):
    kk = pl.program_id(2)

    @pl.when(kk == 0)
    def _():
        acc_ref[...] = jnp.zeros_like(acc_ref)

    acc_ref[...] += _dot(a_ref[...], w_ref[...])

    @pl.when(kk == nk - 1)
    def _():
        o_ref[...] = x_ref[...] + gt_ref[0] * acc_ref[...]


def _mlp_down_residual(a, w, x2, gt, seq, bm=1024, bn=1024, bk=2048):
    t, k = a.shape
    d = w.shape[1]
    bm, bn, bk = min(bm, seq), min(bn, d), min(bk, k)
    per_seq = seq // bm
    nk = k // bk
    return pl.pallas_call(
        functools.partial(_mm_down_kernel, nk=nk),
        grid=(t // bm, d // bn, nk),
        in_specs=[
            pl.BlockSpec((bm, bk), lambda i, j, kk: (i, kk)),
            pl.BlockSpec((bk, bn), lambda i, j, kk: (kk, j)),
            pl.BlockSpec((bm, bn), lambda i, j, kk: (i, j)),
            pl.BlockSpec((1, 1, bn), lambda i, j, kk: (i // per_seq, 0, j)),
        ],
        out_specs=pl.BlockSpec((bm, bn), lambda i, j, kk: (i, j)),
        out_shape=jax.ShapeDtypeStruct((t, d), F32),
        scratch_shapes=[pltpu.VMEM((bm, bn), F32)],
        compiler_params=_params(("parallel", "parallel", "arbitrary")),
        name="mlp_down_residual",
    )(a, w, x2, gt)


def _split3(x):
    hi = x.astype(BF16)
    r = x - hi.astype(F32)
    mid = r.astype(BF16)
    lo = (r - mid.astype(F32)).astype(BF16)
    return hi, mid, lo


def _gate_kernel(ba_ref, alog_ref, dtb_ref, col_ref, row_ref, *, hb, chunk):
    bt, lanes = ba_ref.shape
    x = ba_ref[...]
    beta = jax.nn.sigmoid(x)
    z = x + dtb_ref[...]
    softplus = jnp.maximum(z, 0.0) + jnp.log1p(jnp.exp(-jnp.abs(z)))
    g = -jnp.exp(alog_ref[...]) * softplus

    ri = lax.broadcasted_iota(jnp.int32, (chunk, chunk), 0)
    ci = lax.broadcasted_iota(jnp.int32, (chunk, chunk), 1)
    tril = (ri >= ci).astype(BF16)
    quantity = lax.broadcasted_iota(jnp.int32, (chunk, lanes), 1) // hb

    for c in range(bt // chunk):
        rows = slice(c * chunk, (c + 1) * chunk)
        hi, mid, lo = _split3(g[rows])
        gc = _dot(tril, hi) + _dot(tril, mid) + _dot(tril, lo)
        gl = gc[chunk - 1:chunk, :]
        out = jnp.where(quantity == 0, beta[rows],
              jnp.where(quantity == 1, gc,
              jnp.where(quantity == 2, jnp.exp(gc),
              jnp.where(quantity == 3, jnp.exp(gl - gc),
              jnp.where(quantity == 4, jnp.exp(jnp.broadcast_to(gl, gc.shape)), 0.0)))))
        col_ref[rows, :] = out
        row_ref[:, rows] = out.T


def _gates(ba, alog_l, dtb_l, hb, chunk):
    t, width = ba.shape
    n_hg = width // V7X_LANES
    bt = min(1024, t)
    return pl.pallas_call(
        functools.partial(_gate_kernel, hb=hb, chunk=chunk),
        grid=(n_hg, t // bt),
        in_specs=[
            pl.BlockSpec((bt, V7X_LANES), lambda h, i: (i, h)),
            pl.BlockSpec((1, V7X_LANES), lambda h, i: (0, h)),
            pl.BlockSpec((1, V7X_LANES), lambda h, i: (0, h)),
        ],
        out_specs=[
            pl.BlockSpec((bt, V7X_LANES), lambda h, i: (i, h)),
            pl.BlockSpec((V7X_LANES, bt), lambda h, i: (h, i)),
        ],
        out_shape=[
            jax.ShapeDtypeStruct((t, width), F32),
            jax.ShapeDtypeStruct((width, t), F32),
        ],
        compiler_params=_params(("parallel", "parallel")),
        name="delta_gates",
    )(ba, alog_l, dtb_l)


def _unit_lower_inverse(low, blk_mask, eye):
    c = low.shape[0]
    nblk = c // SOLVE_BASE
    diag = jnp.where(blk_mask, low, 0.0)
    rest = low - diag
    mult = jnp.concatenate(
        [pltpu.roll(diag[b * SOLVE_BASE:(b + 1) * SOLVE_BASE, :], (c - b * SOLVE_BASE) % c, 1)
         for b in range(nblk)], axis=0)
    x = eye
    for j in range(SOLVE_BASE - 1):
        rowj = jnp.broadcast_to(x.reshape(nblk, SOLVE_BASE, c)[:, j:j + 1, :], (nblk, SOLVE_BASE, c))
        x = x - mult[:, j:j + 1] * rowj.reshape(c, c)
    t16 = x.astype(BF16)
    z = _dot(t16, rest.astype(BF16))
    z16 = z.astype(BF16)
    inv = x
    powers = []
    zp16 = z16
    span = 2
    while span < nblk:
        zp16 = _dot(zp16, zp16).astype(BF16)
        powers.append(zp16)
        span *= 2
    for zp in reversed(powers):
        inv = inv + _dot(zp, inv.astype(BF16))
    inv = inv - _dot(z16, inv.astype(BF16))
    return inv


def _delta_kernel(q_ref, k_ref, v_ref, z_ref, cwq_ref, cwk_ref, cwv_ref, gcol_ref, grow_ref, ng_ref,
                  o_ref, qbuf, kbuf, vbuf, s_ref, *, hb, dk, kconv):
    c = q_ref.shape[0]
    pad = V7X_SUBLANES
    first = pl.program_id(2) == 0

    @pl.when(first)
    def _():
        for buf in (qbuf, kbuf, vbuf):
            buf[0:pad, :] = jnp.zeros((pad, buf.shape[1]), F32)
        s_ref[...] = jnp.zeros_like(s_ref)

    def conv_silu(x_ref, buf, cw_ref):
        cur = x_ref[...]
        buf[pad:pad + c, :] = cur
        cw = cw_ref[...]
        y = cur * cw[kconv - 1:kconv, :]
        for s in range(1, kconv):
            y = y + buf[pl.ds(pad - s, c), :] * cw[kconv - 1 - s:kconv - s, :]
        buf[0:pad, :] = cur[c - pad:c, :]
        return _silu(y)

    qc = conv_silu(q_ref, qbuf, cwq_ref)
    kc = conv_silu(k_ref, kbuf, cwk_ref)
    vc = conv_silu(v_ref, vbuf, cwv_ref)

    ri = lax.broadcasted_iota(jnp.int32, (c, c), 0)
    ci = lax.broadcasted_iota(jnp.int32, (c, c), 1)
    causal = ri >= ci
    strict = ri > ci
    blk_mask = (ri // SOLVE_BASE) == (ci // SOLVE_BASE)
    eye = (ri == ci).astype(F32)
    gcol = gcol_ref[...]
    grow = grow_ref[...]
    ng = ng_ref[...]

    for h in range(hb):
        cols = slice(h * dk, (h + 1) * dk)
        qh, kh, vh = qc[:, cols], kc[:, cols], vc[:, cols]
        qn = qh * lax.rsqrt(jnp.sum(qh * qh, axis=-1, keepdims=True) + EPS) * (dk ** -0.5)
        kn = kh * lax.rsqrt(jnp.sum(kh * kh, axis=-1, keepdims=True) + EPS)

        def col(quantity):
            lane = quantity * hb + h
            return gcol[:, lane:lane + 1]

        beta, gc, eg, ekd = col(0), col(1), col(2), col(3)
        egl = gcol[0:1, 4 * hb + h:4 * hb + h + 1]
        gr = grow[hb + h:hb + h + 1, :]
        decay = jnp.where(causal, jnp.exp(jnp.where(causal, gc - gr, 0.0)), 0.0)

        kb = kn * beta
        kn16 = kn.astype(BF16)
        kk = _dot_nt(kb.astype(BF16), kn16)
        qk = _dot_nt(qn.astype(BF16), kn16) * decay
        low = jnp.where(strict, kk * decay, 0.0)
        ainv16 = _unit_lower_inverse(low, blk_mask, eye).astype(BF16)
        u = _dot(ainv16, (vh * beta).astype(BF16))
        w = _dot(ainv16, (kb * eg).astype(BF16))

        s = s_ref[h]
        s16 = s.astype(BF16)
        vnew = u - _dot(w.astype(BF16), s16)
        vn16 = vnew.astype(BF16)
        o = _dot((qn * eg).astype(BF16), s16) + _dot(qk.astype(BF16), vn16)
        s_ref[h] = s * egl + _dot_tn((kn * ekd).astype(BF16), vn16)

        on = o * lax.rsqrt(jnp.mean(o * o, axis=-1, keepdims=True) + EPS) * ng
        o_ref[:, cols] = (on * _silu(z_ref[:, cols])).astype(o_ref.dtype)


def _delta_rule(proj, conv_w, gcol, grow, dn_norm_g, bsz, seq, n_heads, dk, hb):
    t = proj.shape[0]
    c = DELTA_CHUNK
    nc = seq // c
    kconv = conv_w.shape[0]
    wblk = hb * dk
    n_hg = n_heads // hb
    width = n_heads * dk

    def tok(b, hg, n):
        return b * nc + n

    def proj_spec(part):
        return pl.BlockSpec((c, wblk), lambda b, hg, n: (tok(b, hg, n), part * n_hg + hg))

    def conv_spec(part):
        return pl.BlockSpec((kconv, wblk), lambda b, hg, n: (0, part * n_hg + hg))

    return pl.pallas_call(
        functools.partial(_delta_kernel, hb=hb, dk=dk, kconv=kconv),
        grid=(bsz, n_hg, nc),
        in_specs=[
            proj_spec(0), proj_spec(1), proj_spec(2), proj_spec(3),
            conv_spec(0), conv_spec(1), conv_spec(2),
            pl.BlockSpec((c, V7X_LANES), lambda b, hg, n: (tok(b, hg, n), hg)),
            pl.BlockSpec((V7X_LANES, c), lambda b, hg, n: (hg, tok(b, hg, n))),
            pl.BlockSpec((1, dk), lambda b, hg, n: (0, 0)),
        ],
        out_specs=pl.BlockSpec((c, wblk), lambda b, hg, n: (tok(b, hg, n), hg)),
        out_shape=jax.ShapeDtypeStruct((t, width), BF16),
        scratch_shapes=[
            pltpu.VMEM((c + V7X_SUBLANES, wblk), F32),
            pltpu.VMEM((c + V7X_SUBLANES, wblk), F32),
            pltpu.VMEM((c + V7X_SUBLANES, wblk), F32),
            pltpu.VMEM((hb, dk, dk), F32),
        ],
        compiler_params=_params(("parallel", "parallel", "arbitrary")),
        name="gated_delta_rule",
    )(proj, proj, proj, proj, conv_w, conv_w, conv_w, gcol, grow, dn_norm_g.reshape(1, dk))


def _pool_kernel(u_ref, w_ref, sc_ref, o_ref, buf, *, gw, hist):
    bt = u_ref.shape[0]
    n = pl.program_id(1)

    @pl.when(n == 0)
    def _():
        buf[0:hist, :] = jnp.zeros((hist, buf.shape[1]), F32)

    cur = u_ref[...]
    buf[hist:hist + bt, :] = cur
    pos = n * bt + lax.broadcasted_iota(jnp.int32, (bt, 1), 0)
    for g, window in enumerate(POOL_WINDOWS):
        cols = slice(g * gw, (g + 1) * gw)
        cg = cur[:, cols]
        acc = cg
        for d in range(1, window):
            acc = acc + buf[pl.ds(hist - d, bt), cols]
        count = jnp.minimum(pos + 1, window).astype(F32)
        pooled = acc / count - cg
        p = _dot(pooled.astype(BF16), w_ref[g])
        p = p * lax.rsqrt(jnp.mean(p * p, axis=-1, keepdims=True) + EPS)
        o_ref[:, cols] = (p * sc_ref[:, cols]).astype(o_ref.dtype)
    buf[0:hist, :] = cur[bt - hist:bt, :]


def _multiscale_pool(proj, pool_w16, pool_scale, bsz, seq, col_block):
    t = proj.shape[0]
    ng, gw, _ = pool_w16.shape
    width = ng * gw
    hist = max(POOL_WINDOWS)
    bt = min(256, seq)
    per_seq = seq // bt
    return pl.pallas_call(
        functools.partial(_pool_kernel, gw=gw, hist=hist),
        grid=(bsz, per_seq),
        in_specs=[
            pl.BlockSpec((bt, width), lambda b, n: (b * per_seq + n, col_block)),
            pl.BlockSpec((ng, gw, gw), lambda b, n: (0, 0, 0)),
            pl.BlockSpec((1, width), lambda b, n: (0, 0)),
        ],
        out_specs=pl.BlockSpec((bt, width), lambda b, n: (b * per_seq + n, 0)),
        out_shape=jax.ShapeDtypeStruct((t, width), BF16),
        scratch_shapes=[pltpu.VMEM((bt + hist, width), F32)],
        compiler_params=_params(("parallel", "arbitrary")),
        name="multiscale_pool",
    )(proj, pool_w16, pool_scale.reshape(1, width))


def _heads_per_step(n_heads):
    hb = min(n_heads, 8)
    while n_heads % hb:
        hb -= 1
    return hb


def _hybrid_layer(x2, bsz, seq, c, w_ada, b_ada, norm1_g, w_in, conv_w, a_log, dt_bias, dn_norm_g,
                  pool_w, pool_scale, w_out, norm2_g, w_ff1, w_ff2):
    d = x2.shape[1]
    n_heads = a_log.shape[0]
    dk = dn_norm_g.shape[0]
    dn_width = n_heads * dk
    pool_width = pool_scale.shape[0]
    hb = _heads_per_step(n_heads)
    n_hg = n_heads // hb
    assert N_GATE_QUANTITIES * hb <= V7X_LANES
    assert pool_width == dn_width and seq % DELTA_CHUNK == 0

    mod = _modulation(c, w_ada, b_ada)
    sh1, sc1, gt1, sh2, sc2, gt2 = [m.reshape(bsz, 1, d) for m in jnp.split(mod, N_MOD, axis=-1)]

    o2 = 4 * dn_width
    o3 = o2 + n_heads
    o4 = o3 + n_heads
    w_main = jnp.concatenate([w_in[:, :o2], w_in[:, o4:]], axis=1).astype(BF16)
    w_b = w_in[:, o2:o3].reshape(d, n_hg, hb)
    w_a = w_in[:, o3:o4].reshape(d, n_hg, hb)
    lane_pad = V7X_LANES - N_GATE_QUANTITIES * hb
    w_ba = jnp.concatenate([w_b] + [w_a] * (N_GATE_QUANTITIES - 1) + [jnp.zeros((d, n_hg, lane_pad), F32)],
                           axis=2).reshape(d, n_hg * V7X_LANES).astype(BF16)

    def lane_pack(v):
        vv = v.astype(F32).reshape(n_hg, hb)
        return jnp.concatenate([jnp.zeros((n_hg, hb), F32)] + [vv] * (N_GATE_QUANTITIES - 1)
                               + [jnp.zeros((n_hg, lane_pad), F32)], axis=1).reshape(1, n_hg * V7X_LANES)

    h1 = _norm_mod(x2, norm1_g, sc1, sh1, seq)
    proj = _matmul(h1, w_main, F32, _mm_kernel, 1024, 1024, "in_proj")
    ba = _matmul(h1, w_ba, F32, _mm_kernel, 1024, n_hg * V7X_LANES, "gate_proj")
    gcol, grow = _gates(ba, lane_pack(a_log), lane_pack(dt_bias), hb, DELTA_CHUNK)

    o = _delta_rule(proj, conv_w, gcol, grow, dn_norm_g, bsz, seq, n_heads, dk, hb)
    p = _multiscale_pool(proj, pool_w.astype(BF16), pool_scale, bsz, seq, o2 // pool_width)

    x1 = _out_proj_residual(o, p, w_out.astype(BF16), x2, gt1, seq)

    h2 = _norm_mod(x1, norm2_g, sc2, sh2, seq)
    a = _matmul(h2, w_ff1.astype(BF16), BF16, _mm_relu2_kernel, 1024, 1024, "mlp_up_relu2")
    return _mlp_down_residual(a, w_ff2.astype(BF16), x1, gt2, seq)


def kernel(x, c, w_ada, b_ada, norm1_g, w_in, conv_w, a_log, dt_bias, dn_norm_g, pool_w, pool_scale,
           w_out, norm2_g, w_ff1, w_ff2, final_norm_g):
    bsz, seq, d = x.shape
    x2 = x.reshape(bsz * seq, d)
    for l in range(w_ada.shape[0]):
        x2 = _hybrid_layer(x2, bsz, seq, c, w_ada[l], b_ada[l], norm1_g[l], w_in[l], conv_w[l], a_log[l],
                           dt_bias[l], dn_norm_g[l], pool_w[l], pool_scale[l], w_out[l], norm2_g[l],
                           w_ff1[l], w_ff2[l])
    return _final_norm(x2, final_norm_g).reshape(bsz, seq, d)
```

```python
import functools

import jax
import jax.numpy as jnp
from jax import lax
from jax.experimental import pallas as pl
from jax.experimental.pallas import tpu as pltpu

F32 = jnp.float32
BF16 = jnp.bfloat16
EPS = 1e-6

V7X_LANES = 128
V7X_SUBLANES = 8
V7X_VMEM_LIMIT_BYTES = 60 * 1024 * 1024

POOL_WINDOWS = (2, 4, 8, 16)
DELTA_CHUNK = 128
SOLVE_BASE = 2 * V7X_SUBLANES
N_MOD = 6
N_GATE_QUANTITIES = 2


def _params(semantics):
    return pltpu.CompilerParams(dimension_semantics=semantics, vmem_limit_bytes=V7X_VMEM_LIMIT_BYTES)


def _silu(x):
    return x * jax.nn.sigmoid(x)


def _dot(a, b):
    return jnp.dot(a, b, preferred_element_type=F32)


def _dot_nt(a, b):
    return lax.dot_general(a, b, (((1,), (1,)), ((), ())), preferred_element_type=F32)


def _dot_tn(a, b):
    return lax.dot_general(a, b, (((0,), (0,)), ((), ())), preferred_element_type=F32)


def _mod_kernel(c_ref, w_ref, b_ref, o_ref):
    a = _silu(c_ref[...]).astype(BF16)
    o_ref[...] = _dot(a, w_ref[...].astype(BF16)) + b_ref[...]


def _modulation(c, w_ada, b_ada):
    bsz, d = c.shape
    n = w_ada.shape[1]
    rows = V7X_SUBLANES
    c_pad = jnp.zeros((rows, d), F32).at[:bsz].set(c)
    bn = min(512, n)
    out = pl.pallas_call(
        _mod_kernel,
        grid=(n // bn,),
        in_specs=[
            pl.BlockSpec((rows, d), lambda j: (0, 0)),
            pl.BlockSpec((d, bn), lambda j: (0, j)),
            pl.BlockSpec((1, bn), lambda j: (0, j)),
        ],
        out_specs=pl.BlockSpec((rows, bn), lambda j: (0, j)),
        out_shape=jax.ShapeDtypeStruct((rows, n), F32),
        compiler_params=_params(("parallel",)),
        name="adaln_modulation",
    )(c_pad, w_ada, b_ada.reshape(1, n))
    return out[:bsz]


def _norm_mod_kernel(x_ref, g_ref, sc_ref, sh_ref, o_ref):
    x = x_ref[...]
    y = x * lax.rsqrt(jnp.mean(x * x, axis=-1, keepdims=True) + EPS)
    y = y * g_ref[...]
    o_ref[...] = (y * (1.0 + sc_ref[0]) + sh_ref[0]).astype(o_ref.dtype)


def _norm_kernel(x_ref, g_ref, o_ref):
    x = x_ref[...]
    y = x * lax.rsqrt(jnp.mean(x * x, axis=-1, keepdims=True) + EPS)
    o_ref[...] = (y * g_ref[...]).astype(o_ref.dtype)


def _norm_mod(x2, g, sc, sh, seq):
    t, d = x2.shape
    bt = min(256, seq)
    per_seq = seq // bt
    return pl.pallas_call(
        _norm_mod_kernel,
        grid=(t // bt,),
        in_specs=[
            pl.BlockSpec((bt, d), lambda i: (i, 0)),
            pl.BlockSpec((1, d), lambda i: (0, 0)),
            pl.BlockSpec((1, 1, d), lambda i: (i // per_seq, 0, 0)),
            pl.BlockSpec((1, 1, d), lambda i: (i // per_seq, 0, 0)),
        ],
        out_specs=pl.BlockSpec((bt, d), lambda i: (i, 0)),
        out_shape=jax.ShapeDtypeStruct((t, d), BF16),
        compiler_params=_params(("parallel",)),
        name="rmsnorm_modulate",
    )(x2, g.reshape(1, d), sc, sh)


def _final_norm(x2, g):
    t, d = x2.shape
    bt = min(256, t)
    return pl.pallas_call(
        _norm_kernel,
        grid=(t // bt,),
        in_specs=[pl.BlockSpec((bt, d), lambda i: (i, 0)), pl.BlockSpec((1, d), lambda i: (0, 0))],
        out_specs=pl.BlockSpec((bt, d), lambda i: (i, 0)),
        out_shape=jax.ShapeDtypeStruct((t, d), F32),
        compiler_params=_params(("parallel",)),
        name="final_rmsnorm",
    )(x2, g.reshape(1, d))


def _mm_kernel(a_ref, w_ref, o_ref):
    o_ref[...] = _dot(a_ref[...], w_ref[...]).astype(o_ref.dtype)


def _mm_relu2_kernel(a_ref, w_ref, o_ref):
    r = jnp.maximum(_dot(a_ref[...], w_ref[...]), 0.0)
    o_ref[...] = (r * r).astype(o_ref.dtype)


def _fit(block, dim):
    block = min(block, dim)
    while dim % block:
        block -= V7X_LANES
    return block


def _matmul(a, w, out_dtype, kernel, bm, bn, name):
    m, k = a.shape
    n = w.shape[1]
    bm, bn = _fit(bm, m), _fit(bn, n)
    return pl.pallas_call(
        kernel,
        grid=(m // bm, n // bn),
        in_specs=[pl.BlockSpec((bm, k), lambda i, j: (i, 0)), pl.BlockSpec((k, bn), lambda i, j: (0, j))],
        out_specs=pl.BlockSpec((bm, bn), lambda i, j: (i, j)),
        out_shape=jax.ShapeDtypeStruct((m, n), out_dtype),
        compiler_params=_params(("parallel", "parallel")),
        name=name,
    )(a, w)


def _mm_out_kernel(o_ref, p_ref, wo_ref, wp_ref, x_ref, gt_ref, out_ref):
    acc = _dot(o_ref[...], wo_ref[...]) + _dot(p_ref[...], wp_ref[...])
    out_ref[...] = x_ref[...] + gt_ref[0] * acc


def _out_proj_residual(o, p, w_out, x2, gt, seq, bm=1024, bn=1024):
    t, ko = o.shape
    kp = p.shape[1]
    d = w_out.shape[1]
    bm, bn = min(bm, seq), min(bn, d)
    per_seq = seq // bm
    assert ko == kp
    return pl.pallas_call(
        _mm_out_kernel,
        grid=(t // bm, d // bn),
        in_specs=[
            pl.BlockSpec((bm, ko), lambda i, j: (i, 0)),
            pl.BlockSpec((bm, kp), lambda i, j: (i, 0)),
            pl.BlockSpec((ko, bn), lambda i, j: (0, j)),
            pl.BlockSpec((kp, bn), lambda i, j: (1, j)),
            pl.BlockSpec((bm, bn), lambda i, j: (i, j)),
            pl.BlockSpec((1, 1, bn), lambda i, j: (i // per_seq, 0, j)),
        ],
        out_specs=pl.BlockSpec((bm, bn), lambda i, j: (i, j)),
        out_shape=jax.ShapeDtypeStruct((t, d), F32),
        compiler_params=_params(("parallel", "parallel")),
        name="out_proj_residual",
    )(o, p, w_out, w_out, x2, gt)


def _mm_down_kernel(a_ref, w_ref, x_ref, gt_ref, o_ref, *, nk):
    kk = pl.program_id(2)
    part = _dot(a_ref[...], w_ref[...])
    if nk == 1:
        o_ref[...] = x_ref[...] + gt_ref[0] * part
        return

    @pl.when(kk == 0)
    def _():
        o_ref[...] = part

    @pl.when(jnp.logical_and(kk > 0, kk < nk - 1))
    def _():
        o_ref[...] += part

    @pl.when(kk == nk - 1)
    def _():
        o_ref[...] = x_ref[...] + gt_ref[0] * (o_ref[...] + part)


def _mlp_down_residual(a, w, x2, gt, seq, bm=1024, bn=1024, bk=4096):
    t, k = a.shape
    d = w.shape[1]
    bm, bn, bk = min(bm, seq), min(bn, d), min(bk, k)
    per_seq = seq // bm
    nk = k // bk
    return pl.pallas_call(
        functools.partial(_mm_down_kernel, nk=nk),
        grid=(t // bm, d // bn, nk),
        in_specs=[
            pl.BlockSpec((bm, bk), lambda i, j, kk: (i, kk)),
            pl.BlockSpec((bk, bn), lambda i, j, kk: (kk, j)),
            pl.BlockSpec((bm, bn), lambda i, j, kk: (i, j)),
            pl.BlockSpec((1, 1, bn), lambda i, j, kk: (i // per_seq, 0, j)),
        ],
        out_specs=pl.BlockSpec((bm, bn), lambda i, j, kk: (i, j)),
        out_shape=jax.ShapeDtypeStruct((t, d), F32),
        compiler_params=_params(("parallel", "parallel", "arbitrary")),
        name="mlp_down_residual",
    )(a, w, x2, gt)


def _split3(x):
    hi = x.astype(BF16)
    r = x - hi.astype(F32)
    mid = r.astype(BF16)
    lo = (r - mid.astype(F32)).astype(BF16)
    return hi, mid, lo


def _gate_kernel(ba_ref, alog_ref, dtb_ref, col_ref, row_ref, *, hb, chunk):
    bt, lanes = ba_ref.shape
    x = ba_ref[...]
    beta = jax.nn.sigmoid(x)
    z = x + dtb_ref[...]
    softplus = jnp.maximum(z, 0.0) + jnp.log1p(jnp.exp(-jnp.abs(z)))
    g = -jnp.exp(alog_ref[...]) * softplus

    ri = lax.broadcasted_iota(jnp.int32, (chunk, chunk), 0)
    ci = lax.broadcasted_iota(jnp.int32, (chunk, chunk), 1)
    tril = (ri >= ci).astype(BF16)
    quantity = lax.broadcasted_iota(jnp.int32, (chunk, lanes), 1) // hb

    for c in range(bt // chunk):
        rows = slice(c * chunk, (c + 1) * chunk)
        hi, mid, lo = _split3(g[rows])
        gc = _dot(tril, hi) + _dot(tril, mid) + _dot(tril, lo)
        out = jnp.where(quantity == 0, beta[rows], jnp.where(quantity == 1, gc, 0.0))
        col_ref[rows, :] = out
        row_ref[:, rows] = out.T


def _gates(ba, alog_l, dtb_l, hb, chunk):
    t, width = ba.shape
    n_hg = width // V7X_LANES
    bt = min(1024, t)
    return pl.pallas_call(
        functools.partial(_gate_kernel, hb=hb, chunk=chunk),
        grid=(n_hg, t // bt),
        in_specs=[
            pl.BlockSpec((bt, V7X_LANES), lambda h, i: (i, h)),
            pl.BlockSpec((1, V7X_LANES), lambda h, i: (0, h)),
            pl.BlockSpec((1, V7X_LANES), lambda h, i: (0, h)),
        ],
        out_specs=[
            pl.BlockSpec((bt, V7X_LANES), lambda h, i: (i, h)),
            pl.BlockSpec((V7X_LANES, bt), lambda h, i: (h, i)),
        ],
        out_shape=[
            jax.ShapeDtypeStruct((t, width), F32),
            jax.ShapeDtypeStruct((width, t), F32),
        ],
        compiler_params=_params(("parallel", "parallel")),
        name="delta_gates",
    )(ba, alog_l, dtb_l)


def _unit_lower_inverses(lows, blk_mask, eye):
    c = lows[0].shape[0]
    nblk = c // SOLVE_BASE
    diags = [jnp.where(blk_mask, low, 0.0) for low in lows]
    rests = [(low - diag).astype(BF16) for low, diag in zip(lows, diags)]

    sub = lax.broadcasted_iota(jnp.int32, (SOLVE_BASE, c), 0)
    lane = lax.broadcasted_iota(jnp.int32, (SOLVE_BASE, c), 1)
    blk_start = (lane // SOLVE_BASE) * SOLVE_BASE
    compact_eye = (sub == lane - blk_start).astype(F32)

    def block_inverse(diag):
        lc = jnp.sum(diag.reshape(nblk, SOLVE_BASE, c), axis=0)
        x = compact_eye
        for j in range(SOLVE_BASE - 1):
            mult = jnp.take_along_axis(lc, blk_start + j, axis=1)
            x = x - mult * jnp.broadcast_to(x[j:j + 1, :], (SOLVE_BASE, c))
        return jnp.where(blk_mask, jnp.tile(x, (nblk, 1)), 0.0)

    invs = [block_inverse(diag) for diag in diags]
    zps = [_dot(inv.astype(BF16), rest).astype(BF16) for inv, rest in zip(invs, rests)]
    invs = [inv - _dot(zp, inv.astype(BF16)) for zp, inv in zip(zps, invs)]
    span = 2
    while span < nblk:
        zps = [_dot(zp, zp).astype(BF16) for zp in zps]
        invs = [inv + _dot(zp, inv.astype(BF16)) for zp, inv in zip(zps, invs)]
        span *= 2
    return invs


def _delta_kernel(q_ref, k_ref, v_ref, z_ref, cwq_ref, cwk_ref, cwv_ref, gcol_ref, grow_ref, ng_ref,
                  o_ref, qbuf, kbuf, vbuf, s_ref, *, hb, dk, kconv):
    c = q_ref.shape[0]
    pad = V7X_SUBLANES
    first = pl.program_id(2) == 0

    @pl.when(first)
    def _():
        for buf in (qbuf, kbuf, vbuf):
            buf[0:pad, :] = jnp.zeros((pad, buf.shape[1]), F32)
        s_ref[...] = jnp.zeros_like(s_ref)

    def conv_silu(x_ref, buf, cw_ref):
        cur = x_ref[...]
        buf[pad:pad + c, :] = cur
        cw = cw_ref[...]
        y = cur * cw[kconv - 1:kconv, :]
        for s in range(1, kconv):
            y = y + buf[pl.ds(pad - s, c), :] * cw[kconv - 1 - s:kconv - s, :]
        buf[0:pad, :] = cur[c - pad:c, :]
        return _silu(y)

    qc = conv_silu(q_ref, qbuf, cwq_ref)
    kc = conv_silu(k_ref, kbuf, cwk_ref)
    vc = conv_silu(v_ref, vbuf, cwv_ref)

    ri = lax.broadcasted_iota(jnp.int32, (c, c), 0)
    ci = lax.broadcasted_iota(jnp.int32, (c, c), 1)
    causal = ri >= ci
    strict = ri > ci
    blk_mask = (ri // SOLVE_BASE) == (ci // SOLVE_BASE)
    eye = (ri == ci).astype(F32)
    gcol = gcol_ref[...]
    grow = grow_ref[...]
    ng = ng_ref[...]

    heads = range(hb)

    def head_cols(h):
        return slice(h * dk, (h + 1) * dk)

    def gate_col(quantity, h):
        lane = quantity * hb + h
        return gcol[:, lane:lane + 1]

    qn, kn, kb, rhs, decay, qd16, kd16, egl = [], [], [], [], [], [], [], []
    for h in heads:
        qh, kh, vh = qc[:, head_cols(h)], kc[:, head_cols(h)], vc[:, head_cols(h)]
        qn_h = qh * lax.rsqrt(jnp.sum(qh * qh, axis=-1, keepdims=True) + EPS) * (dk ** -0.5)
        kn_h = kh * lax.rsqrt(jnp.sum(kh * kh, axis=-1, keepdims=True) + EPS)
        beta = jnp.broadcast_to(gate_col(0, h), (c, dk))
        gc = jnp.broadcast_to(gate_col(1, h), (c, dk))
        gl = gc[c - 1:c, :]
        eg = jnp.exp(gc)
        ekd = jnp.exp(gl - gc)
        egl.append(jnp.exp(gl))
        gr = grow[hb + h:hb + h + 1, :]
        gc_cc = gc if c == dk else jnp.broadcast_to(gate_col(1, h), (c, c))
        decay.append(jnp.where(causal, jnp.exp(jnp.where(causal, gc_cc - gr, 0.0)), 0.0))
        kb_h = kn_h * beta
        qn.append(qn_h.astype(BF16))
        kn.append(kn_h.astype(BF16))
        kb.append(kb_h.astype(BF16))
        rhs.append(jnp.concatenate([(vh * beta).astype(BF16), (kb_h * eg).astype(BF16)], axis=1))
        qd16.append((qn_h * eg).astype(BF16))
        kd16.append((kn_h * ekd).astype(BF16))

    kk = [_dot_nt(kb[h], kn[h]) for h in heads]
    qk16 = [(_dot_nt(qn[h], kn[h]) * decay[h]).astype(BF16) for h in heads]
    lows = [jnp.where(strict, kk[h] * decay[h], 0.0) for h in heads]
    ainv = _unit_lower_inverses(lows, blk_mask, eye)
    uw = [_dot(ainv[h].astype(BF16), rhs[h]) for h in heads]
    s = [s_ref[h] for h in heads]
    s16 = [s[h].astype(BF16) for h in heads]
    ws_qs = [_dot(jnp.concatenate([uw[h][:, dk:].astype(BF16), qd16[h]], axis=0), s16[h]) for h in heads]
    vn16 = [(uw[h][:, :dk] - ws_qs[h][:c]).astype(BF16) for h in heads]
    o = [ws_qs[h][c:] + _dot(qk16[h], vn16[h]) for h in heads]
    for h in heads:
        s_ref[h] = s[h] * egl[h] + _dot_tn(kd16[h], vn16[h])
    for h in heads:
        on = o[h] * lax.rsqrt(jnp.mean(o[h] * o[h], axis=-1, keepdims=True) + EPS) * ng
        o_ref[:, head_cols(h)] = (on * _silu(z_ref[:, head_cols(h)])).astype(o_ref.dtype)


def _delta_rule(proj, conv_w, gcol, grow, dn_norm_g, bsz, seq, n_heads, dk, hb):
    t = proj.shape[0]
    c = DELTA_CHUNK
    nc = seq // c
    kconv = conv_w.shape[0]
    wblk = hb * dk
    n_hg = n_heads // hb
    width = n_heads * dk

    def tok(b, hg, n):
        return b * nc + n

    def proj_spec(part):
        return pl.BlockSpec((c, wblk), lambda b, hg, n: (tok(b, hg, n), part * n_hg + hg))

    def conv_spec(part):
        return pl.BlockSpec((kconv, wblk), lambda b, hg, n: (0, part * n_hg + hg))

    return pl.pallas_call(
        functools.partial(_delta_kernel, hb=hb, dk=dk, kconv=kconv),
        grid=(bsz, n_hg, nc),
        in_specs=[
            proj_spec(0), proj_spec(1), proj_spec(2), proj_spec(3),
            conv_spec(0), conv_spec(1), conv_spec(2),
            pl.BlockSpec((c, V7X_LANES), lambda b, hg, n: (tok(b, hg, n), hg)),
            pl.BlockSpec((V7X_LANES, c), lambda b, hg, n: (hg, tok(b, hg, n))),
            pl.BlockSpec((1, dk), lambda b, hg, n: (0, 0)),
        ],
        out_specs=pl.BlockSpec((c, wblk), lambda b, hg, n: (tok(b, hg, n), hg)),
        out_shape=jax.ShapeDtypeStruct((t, width), BF16),
        scratch_shapes=[
            pltpu.VMEM((c + V7X_SUBLANES, wblk), F32),
            pltpu.VMEM((c + V7X_SUBLANES, wblk), F32),
            pltpu.VMEM((c + V7X_SUBLANES, wblk), F32),
            pltpu.VMEM((hb, dk, dk), F32),
        ],
        compiler_params=_params(("parallel", "parallel", "arbitrary")),
        name="gated_delta_rule",
    )(proj, proj, proj, proj, conv_w, conv_w, conv_w, gcol, grow, dn_norm_g.reshape(1, dk))


def _pool_kernel(u_ref, w_ref, sc_ref, o_ref, buf, *, gw, hist):
    bt = u_ref.shape[0]
    n = pl.program_id(1)

    @pl.when(n == 0)
    def _():
        buf[0:hist, :] = jnp.zeros((hist, buf.shape[1]), F32)

    cur = u_ref[...]
    buf[hist:hist + bt, :] = cur
    pos = n * bt + lax.broadcasted_iota(jnp.int32, (bt, 1), 0)
    for g, window in enumerate(POOL_WINDOWS):
        cols = slice(g * gw, (g + 1) * gw)
        cg = cur[:, cols]
        acc = cg
        for d in range(1, window):
            acc = acc + buf[pl.ds(hist - d, bt), cols]
        count = jnp.minimum(pos + 1, window).astype(F32)
        pooled = acc / count - cg
        p = _dot(pooled.astype(BF16), w_ref[g])
        p = p * lax.rsqrt(jnp.mean(p * p, axis=-1, keepdims=True) + EPS)
        o_ref[:, cols] = (p * sc_ref[:, cols]).astype(o_ref.dtype)
    buf[0:hist, :] = cur[bt - hist:bt, :]


def _multiscale_pool(pool_in, pool_w16, pool_scale, bsz, seq):
    t = pool_in.shape[0]
    ng, gw, _ = pool_w16.shape
    width = ng * gw
    hist = max(POOL_WINDOWS)
    bt = min(256, seq)
    per_seq = seq // bt
    return pl.pallas_call(
        functools.partial(_pool_kernel, gw=gw, hist=hist),
        grid=(bsz, per_seq),
        in_specs=[
            pl.BlockSpec((bt, width), lambda b, n: (b * per_seq + n, 0)),
            pl.BlockSpec((ng, gw, gw), lambda b, n: (0, 0, 0)),
            pl.BlockSpec((1, width), lambda b, n: (0, 0)),
        ],
        out_specs=pl.BlockSpec((bt, width), lambda b, n: (b * per_seq + n, 0)),
        out_shape=jax.ShapeDtypeStruct((t, width), BF16),
        scratch_shapes=[pltpu.VMEM((bt + hist, width), F32)],
        compiler_params=_params(("parallel", "arbitrary")),
        name="multiscale_pool",
    )(pool_in, pool_w16, pool_scale.reshape(1, width))


def _heads_per_step(n_heads):
    hb = min(n_heads, 8)
    while n_heads % hb:
        hb -= 1
    return hb


def _hybrid_layer(x2, bsz, seq, c, w_ada, b_ada, norm1_g, w_in, conv_w, a_log, dt_bias, dn_norm_g,
                  pool_w, pool_scale, w_out, norm2_g, w_ff1, w_ff2):
    d = x2.shape[1]
    n_heads = a_log.shape[0]
    dk = dn_norm_g.shape[0]
    dn_width = n_heads * dk
    pool_width = pool_scale.shape[0]
    hb = _heads_per_step(n_heads)
    n_hg = n_heads // hb
    assert N_GATE_QUANTITIES * hb <= V7X_LANES
    assert pool_width == dn_width and seq % DELTA_CHUNK == 0

    mod = _modulation(c, w_ada, b_ada)
    sh1, sc1, gt1, sh2, sc2, gt2 = [m.reshape(bsz, 1, d) for m in jnp.split(mod, N_MOD, axis=-1)]

    o2 = 4 * dn_width
    o3 = o2 + n_heads
    o4 = o3 + n_heads
    w_qkvz = w_in[:, :o2].astype(BF16)
    w_pool = w_in[:, o4:].astype(BF16)
    w_b = w_in[:, o2:o3].reshape(d, n_hg, hb)
    w_a = w_in[:, o3:o4].reshape(d, n_hg, hb)
    lane_pad = V7X_LANES - N_GATE_QUANTITIES * hb
    w_ba = jnp.concatenate([w_b] + [w_a] * (N_GATE_QUANTITIES - 1) + [jnp.zeros((d, n_hg, lane_pad), F32)],
                           axis=2).reshape(d, n_hg * V7X_LANES).astype(BF16)

    def lane_pack(v):
        vv = v.astype(F32).reshape(n_hg, hb)
        return jnp.concatenate([jnp.zeros((n_hg, hb), F32)] + [vv] * (N_GATE_QUANTITIES - 1)
                               + [jnp.zeros((n_hg, lane_pad), F32)], axis=1).reshape(1, n_hg * V7X_LANES)

    h1 = _norm_mod(x2, norm1_g, sc1, sh1, seq)
    proj = _matmul(h1, w_qkvz, F32, _mm_kernel, 1024, 1024, "in_proj")
    pool_in = _matmul(h1, w_pool, F32, _mm_kernel, 1024, 1024, "pool_proj")
    ba = _matmul(h1, w_ba, F32, _mm_kernel, 1024, n_hg * V7X_LANES, "gate_proj")
    gcol, grow = _gates(ba, lane_pack(a_log), lane_pack(dt_bias), hb, DELTA_CHUNK)

    o = _delta_rule(proj, conv_w, gcol, grow, dn_norm_g, bsz, seq, n_heads, dk, hb)
    p = _multiscale_pool(pool_in, pool_w.astype(BF16), pool_scale, bsz, seq)

    x1 = _out_proj_residual(o, p, w_out.astype(BF16), x2, gt1, seq)

    h2 = _norm_mod(x1, norm2_g, sc2, sh2, seq)
    a = _matmul(h2, w_ff1.astype(BF16), BF16, _mm_relu2_kernel, 1024, 1024, "mlp_up_relu2")
    return _mlp_down_residual(a, w_ff2.astype(BF16), x1, gt2, seq)


def kernel(x, c, w_ada, b_ada, norm1_g, w_in, conv_w, a_log, dt_bias, dn_norm_g, pool_w, pool_scale,
           w_out, norm2_g, w_ff1, w_ff2, final_norm_g):
    bsz, seq, d = x.shape
    x2 = x.reshape(bsz * seq, d)
    for l in range(w_ada.shape[0]):
        x2 = _hybrid_layer(x2, bsz, seq, c, w_ada[l], b_ada[l], norm1_g[l], w_in[l], conv_w[l], a_log[l],
                           dt_bias[l], dn_norm_g[l], pool_w[l], pool_scale[l], w_out[l], norm2_g[l],
                           w_ff1[l], w_ff2[l])
    return _final_norm(x2, final_norm_g).reshape(bsz, seq, d)
```

```python
import functools

import jax
import jax.numpy as jnp
from jax import lax
from jax.experimental import pallas as pl
from jax.experimental.pallas import tpu as pltpu

F32 = jnp.float32
BF16 = jnp.bfloat16
EPS = 1e-6

V7X_LANES = 128
V7X_SUBLANES = 8
V7X_VMEM_LIMIT_BYTES = 60 * 1024 * 1024

POOL_WINDOWS = (2, 4, 8, 16)
DELTA_CHUNK = 128
SOLVE_BASE = 2 * V7X_SUBLANES
N_MOD = 6
N_GATE_QUANTITIES = 2


def _params(semantics):
    return pltpu.CompilerParams(dimension_semantics=semantics, vmem_limit_bytes=V7X_VMEM_LIMIT_BYTES)


def _silu(x):
    return x * jax.nn.sigmoid(x)


def _dot(a, b):
    return jnp.dot(a, b, preferred_element_type=F32)


def _dot_nt(a, b):
    return lax.dot_general(a, b, (((1,), (1,)), ((), ())), preferred_element_type=F32)


def _dot_tn(a, b):
    return lax.dot_general(a, b, (((0,), (0,)), ((), ())), preferred_element_type=F32)


def _mod_kernel(c_ref, w_ref, b_ref, o_ref):
    a = _silu(c_ref[...]).astype(BF16)
    o_ref[...] = _dot(a, w_ref[...].astype(BF16)) + b_ref[...]


def _modulation(c, w_ada, b_ada):
    bsz, d = c.shape
    n = w_ada.shape[1]
    rows = V7X_SUBLANES
    c_pad = jnp.zeros((rows, d), F32).at[:bsz].set(c)
    bn = min(512, n)
    out = pl.pallas_call(
        _mod_kernel,
        grid=(n // bn,),
        in_specs=[
            pl.BlockSpec((rows, d), lambda j: (0, 0)),
            pl.BlockSpec((d, bn), lambda j: (0, j)),
            pl.BlockSpec((1, bn), lambda j: (0, j)),
        ],
        out_specs=pl.BlockSpec((rows, bn), lambda j: (0, j)),
        out_shape=jax.ShapeDtypeStruct((rows, n), F32),
        compiler_params=_params(("parallel",)),
        name="adaln_modulation",
    )(c_pad, w_ada, b_ada.reshape(1, n))
    return out[:bsz]


def _norm_mod_kernel(x_ref, g_ref, sc_ref, sh_ref, o_ref):
    x = x_ref[...]
    y = x * lax.rsqrt(jnp.mean(x * x, axis=-1, keepdims=True) + EPS)
    y = y * g_ref[...]
    o_ref[...] = (y * (1.0 + sc_ref[0]) + sh_ref[0]).astype(o_ref.dtype)


def _norm_kernel(x_ref, g_ref, o_ref):
    x = x_ref[...]
    y = x * lax.rsqrt(jnp.mean(x * x, axis=-1, keepdims=True) + EPS)
    o_ref[...] = (y * g_ref[...]).astype(o_ref.dtype)


def _norm_mod(x2, g, sc, sh, seq):
    t, d = x2.shape
    bt = min(256, seq)
    per_seq = seq // bt
    return pl.pallas_call(
        _norm_mod_kernel,
        grid=(t // bt,),
        in_specs=[
            pl.BlockSpec((bt, d), lambda i: (i, 0)),
            pl.BlockSpec((1, d), lambda i: (0, 0)),
            pl.BlockSpec((1, 1, d), lambda i: (i // per_seq, 0, 0)),
            pl.BlockSpec((1, 1, d), lambda i: (i // per_seq, 0, 0)),
        ],
        out_specs=pl.BlockSpec((bt, d), lambda i: (i, 0)),
        out_shape=jax.ShapeDtypeStruct((t, d), BF16),
        compiler_params=_params(("parallel",)),
        name="rmsnorm_modulate",
    )(x2, g.reshape(1, d), sc, sh)


def _final_norm(x2, g):
    t, d = x2.shape
    bt = min(256, t)
    return pl.pallas_call(
        _norm_kernel,
        grid=(t // bt,),
        in_specs=[pl.BlockSpec((bt, d), lambda i: (i, 0)), pl.BlockSpec((1, d), lambda i: (0, 0))],
        out_specs=pl.BlockSpec((bt, d), lambda i: (i, 0)),
        out_shape=jax.ShapeDtypeStruct((t, d), F32),
        compiler_params=_params(("parallel",)),
        name="final_rmsnorm",
    )(x2, g.reshape(1, d))


def _mm_kernel(a_ref, w_ref, o_ref):
    o_ref[...] = _dot(a_ref[...], w_ref[...]).astype(o_ref.dtype)


def _fit(block, dim):
    block = min(block, dim)
    while dim % block:
        block -= V7X_LANES
    return block


def _matmul(a, w, out_dtype, kernel, bm, bn, name):
    m, k = a.shape
    n = w.shape[1]
    bm, bn = _fit(bm, m), _fit(bn, n)
    return pl.pallas_call(
        kernel,
        grid=(m // bm, n // bn),
        in_specs=[pl.BlockSpec((bm, k), lambda i, j: (i, 0)), pl.BlockSpec((k, bn), lambda i, j: (0, j))],
        out_specs=pl.BlockSpec((bm, bn), lambda i, j: (i, j)),
        out_shape=jax.ShapeDtypeStruct((m, n), out_dtype),
        compiler_params=_params(("parallel", "parallel")),
        name=name,
    )(a, w)


def _stage_weight_chunk(wchunk_ref, wslots, n_col_blocks):
    p, i = pl.program_id(0), pl.program_id(1)
    rows = wchunk_ref.shape[0]

    @pl.when(p < n_col_blocks)
    def _():
        start = pl.multiple_of(i * rows, rows)
        wslots[p % 2, pl.ds(start, rows), :] = wchunk_ref[...].astype(BF16)


def _mm_ws_kernel(a_ref, wchunk_ref, o_ref, wslots, *, n_col_blocks, relu2):
    _stage_weight_chunk(wchunk_ref, wslots, n_col_blocks)
    p = pl.program_id(0)

    @pl.when(p > 0)
    def _():
        acc = _dot(a_ref[...], wslots[(p - 1) % 2])
        if relu2:
            acc = jnp.maximum(acc, 0.0)
            acc = acc * acc
        o_ref[...] = acc.astype(o_ref.dtype)


def _mm_ws_out_kernel(o_ref, p_ref, wchunk_ref, x_ref, gt_ref, out_ref, wslots, *, n_col_blocks):
    _stage_weight_chunk(wchunk_ref, wslots, n_col_blocks)
    p = pl.program_id(0)
    ko = o_ref.shape[1]

    @pl.when(p > 0)
    def _():
        slot = (p - 1) % 2
        acc = _dot(o_ref[...], wslots[slot, 0:ko, :]) + _dot(p_ref[...], wslots[slot, ko:, :])
        out_ref[...] = x_ref[...] + gt_ref[0] * acc


def _ws_geometry(m, k, n, bm, bn):
    bm, bn = _fit(bm, m), _fit(bn, n)
    n_row_blocks = m // bm
    rows = k // n_row_blocks
    assert rows * n_row_blocks == k and rows % (2 * V7X_SUBLANES) == 0
    return bm, bn, n_row_blocks, n // bn, rows


def _ws_maps(n_col_blocks):
    def row_block(p, i):
        return jnp.where(p == 0, 0, i)

    def chunk(p, i):
        return (jnp.where(p == n_col_blocks, 0, i), jnp.minimum(p, n_col_blocks - 1))

    def col_block(p, i):
        return jnp.maximum(p - 1, 0)

    return row_block, chunk, col_block


def _matmul_ws(a, w, n, out_dtype, relu2, name, bm=1024, bn=1024):
    m, k = a.shape
    bm, bn, n_row_blocks, n_col_blocks, rows = _ws_geometry(m, k, n, bm, bn)
    row_block, chunk, col_block = _ws_maps(n_col_blocks)
    return pl.pallas_call(
        functools.partial(_mm_ws_kernel, n_col_blocks=n_col_blocks, relu2=relu2),
        grid=(n_col_blocks + 1, n_row_blocks),
        in_specs=[
            pl.BlockSpec((bm, k), lambda p, i: (row_block(p, i), 0)),
            pl.BlockSpec((rows, bn), chunk),
        ],
        out_specs=pl.BlockSpec((bm, bn), lambda p, i: (row_block(p, i), col_block(p, i))),
        out_shape=jax.ShapeDtypeStruct((m, n), out_dtype),
        scratch_shapes=[pltpu.VMEM((2, k, bn), BF16)],
        compiler_params=_params(("arbitrary", "arbitrary")),
        name=name,
    )(a, w)


def _out_proj_residual(o, p, w_out, x2, gt, seq, bm=1024, bn=1024):
    t, ko = o.shape
    k, d = w_out.shape
    assert ko + p.shape[1] == k
    bm, bn, n_row_blocks, n_col_blocks, rows = _ws_geometry(t, k, d, min(bm, seq), bn)
    per_seq = seq // bm
    row_block, chunk, col_block = _ws_maps(n_col_blocks)
    return pl.pallas_call(
        functools.partial(_mm_ws_out_kernel, n_col_blocks=n_col_blocks),
        grid=(n_col_blocks + 1, n_row_blocks),
        in_specs=[
            pl.BlockSpec((bm, ko), lambda p, i: (row_block(p, i), 0)),
            pl.BlockSpec((bm, k - ko), lambda p, i: (row_block(p, i), 0)),
            pl.BlockSpec((rows, bn), chunk),
            pl.BlockSpec((bm, bn), lambda p, i: (row_block(p, i), col_block(p, i))),
            pl.BlockSpec((1, 1, bn), lambda p, i: (row_block(p, i) // per_seq, 0, col_block(p, i))),
        ],
        out_specs=pl.BlockSpec((bm, bn), lambda p, i: (row_block(p, i), col_block(p, i))),
        out_shape=jax.ShapeDtypeStruct((t, d), F32),
        scratch_shapes=[pltpu.VMEM((2, k, bn), BF16)],
        compiler_params=_params(("arbitrary", "arbitrary")),
        name="out_proj_residual",
    )(o, p, w_out, x2, gt)


def _mm_down_kernel(a_ref, w_ref, x_ref, gt_ref, o_ref, *, nk):
    kk = pl.program_id(2)
    part = _dot(a_ref[...], w_ref[...])
    if nk == 1:
        o_ref[...] = x_ref[...] + gt_ref[0] * part
        return

    @pl.when(kk == 0)
    def _():
        o_ref[...] = part

    @pl.when(jnp.logical_and(kk > 0, kk < nk - 1))
    def _():
        o_ref[...] += part

    @pl.when(kk == nk - 1)
    def _():
        o_ref[...] = x_ref[...] + gt_ref[0] * (o_ref[...] + part)


def _mlp_down_residual(a, w, x2, gt, seq, bm=1024, bn=1024, bk=4096):
    t, k = a.shape
    d = w.shape[1]
    bm, bn, bk = min(bm, seq), min(bn, d), min(bk, k)
    per_seq = seq // bm
    nk = k // bk
    return pl.pallas_call(
        functools.partial(_mm_down_kernel, nk=nk),
        grid=(t // bm, d // bn, nk),
        in_specs=[
            pl.BlockSpec((bm, bk), lambda i, j, kk: (i, kk)),
            pl.BlockSpec((bk, bn), lambda i, j, kk: (kk, j)),
            pl.BlockSpec((bm, bn), lambda i, j, kk: (i, j)),
            pl.BlockSpec((1, 1, bn), lambda i, j, kk: (i // per_seq, 0, j)),
        ],
        out_specs=pl.BlockSpec((bm, bn), lambda i, j, kk: (i, j)),
        out_shape=jax.ShapeDtypeStruct((t, d), F32),
        compiler_params=_params(("parallel", "parallel", "arbitrary")),
        name="mlp_down_residual",
    )(a, w, x2, gt)


def _split3(x):
    hi = x.astype(BF16)
    r = x - hi.astype(F32)
    mid = r.astype(BF16)
    lo = (r - mid.astype(F32)).astype(BF16)
    return hi, mid, lo


def _gate_kernel(ba_ref, alog_ref, dtb_ref, col_ref, row_ref, *, hb, chunk):
    bt, lanes = ba_ref.shape
    x = ba_ref[...]
    beta = jax.nn.sigmoid(x)
    z = x + dtb_ref[...]
    softplus = jnp.maximum(z, 0.0) + jnp.log1p(jnp.exp(-jnp.abs(z)))
    g = -jnp.exp(alog_ref[...]) * softplus

    ri = lax.broadcasted_iota(jnp.int32, (chunk, chunk), 0)
    ci = lax.broadcasted_iota(jnp.int32, (chunk, chunk), 1)
    tril = (ri >= ci).astype(BF16)
    quantity = lax.broadcasted_iota(jnp.int32, (chunk, lanes), 1) // hb

    for c in range(bt // chunk):
        rows = slice(c * chunk, (c + 1) * chunk)
        hi, mid, lo = _split3(g[rows])
        gc = _dot(tril, hi) + _dot(tril, mid) + _dot(tril, lo)
        out = jnp.where(quantity == 0, beta[rows], jnp.where(quantity == 1, gc, 0.0))
        col_ref[rows, :] = out
        row_ref[:, rows] = out.T


def _gates(ba, alog_l, dtb_l, hb, chunk):
    t, width = ba.shape
    n_hg = width // V7X_LANES
    bt = min(1024, t)
    return pl.pallas_call(
        functools.partial(_gate_kernel, hb=hb, chunk=chunk),
        grid=(n_hg, t // bt),
        in_specs=[
            pl.BlockSpec((bt, V7X_LANES), lambda h, i: (i, h)),
            pl.BlockSpec((1, V7X_LANES), lambda h, i: (0, h)),
            pl.BlockSpec((1, V7X_LANES), lambda h, i: (0, h)),
        ],
        out_specs=[
            pl.BlockSpec((bt, V7X_LANES), lambda h, i: (i, h)),
            pl.BlockSpec((V7X_LANES, bt), lambda h, i: (h, i)),
        ],
        out_shape=[
            jax.ShapeDtypeStruct((t, width), F32),
            jax.ShapeDtypeStruct((width, t), F32),
        ],
        compiler_params=_params(("parallel", "parallel")),
        name="delta_gates",
    )(ba, alog_l, dtb_l)


def _unit_lower_inverses(lows, blk_mask, eye):
    c = lows[0].shape[0]
    nblk = c // SOLVE_BASE
    diags = [jnp.where(blk_mask, low, 0.0) for low in lows]
    rests = [(low - diag).astype(BF16) for low, diag in zip(lows, diags)]

    sub = lax.broadcasted_iota(jnp.int32, (SOLVE_BASE, c), 0)
    lane = lax.broadcasted_iota(jnp.int32, (SOLVE_BASE, c), 1)
    blk_start = (lane // SOLVE_BASE) * SOLVE_BASE
    compact_eye = (sub == lane - blk_start).astype(F32)

    def block_inverse(diag):
        lc = jnp.sum(diag.reshape(nblk, SOLVE_BASE, c), axis=0)
        x = compact_eye
        for j in range(SOLVE_BASE - 1):
            mult = jnp.take_along_axis(lc, blk_start + j, axis=1)
            x = x - mult * jnp.broadcast_to(x[j:j + 1, :], (SOLVE_BASE, c))
        return jnp.where(blk_mask, jnp.tile(x, (nblk, 1)), 0.0)

    invs = [block_inverse(diag) for diag in diags]
    zps = [_dot(inv.astype(BF16), rest).astype(BF16) for inv, rest in zip(invs, rests)]
    invs = [inv - _dot(zp, inv.astype(BF16)) for zp, inv in zip(zps, invs)]
    span = 2
    while span < nblk:
        zps = [_dot(zp, zp).astype(BF16) for zp in zps]
        invs = [inv + _dot(zp, inv.astype(BF16)) for zp, inv in zip(zps, invs)]
        span *= 2
    return invs


def _delta_kernel(q_ref, k_ref, v_ref, z_ref, cwq_ref, cwk_ref, cwv_ref, gcol_ref, grow_ref, ng_ref,
                  o_ref, qbuf, kbuf, vbuf, s_ref, *, hb, dk, kconv):
    c = q_ref.shape[0]
    pad = V7X_SUBLANES
    first = pl.program_id(2) == 0

    @pl.when(first)
    def _():
        for tail in (qbuf, kbuf, vbuf):
            tail[...] = jnp.zeros_like(tail)
        s_ref[...] = jnp.zeros_like(s_ref)

    def conv_silu(x_ref, tail_ref, cw_ref):
        cur = x_ref[...]
        width = cur.shape[1]
        tiles = jnp.concatenate([tail_ref[...], cur], axis=0).reshape(c // pad + 1, pad, width)
        sublane = lax.broadcasted_iota(jnp.int32, (1, pad, width), 1)
        cw = cw_ref[...]
        y = cur * cw[kconv - 1:kconv, :]
        for s in range(1, kconv):
            rot = pltpu.roll(tiles, s, 1)
            shifted = jnp.where(sublane < s, rot[:-1], rot[1:]).reshape(c, width)
            y = y + shifted * cw[kconv - 1 - s:kconv - s, :]
        tail_ref[...] = cur[c - pad:c, :]
        return _silu(y)

    qc = conv_silu(q_ref, qbuf, cwq_ref)
    kc = conv_silu(k_ref, kbuf, cwk_ref)
    vc = conv_silu(v_ref, vbuf, cwv_ref)

    ri = lax.broadcasted_iota(jnp.int32, (c, c), 0)
    ci = lax.broadcasted_iota(jnp.int32, (c, c), 1)
    causal = ri >= ci
    strict = ri > ci
    blk_mask = (ri // SOLVE_BASE) == (ci // SOLVE_BASE)
    eye = (ri == ci).astype(F32)
    gcol = gcol_ref[...]
    grow = grow_ref[...]
    ng = ng_ref[...]

    heads = range(hb)

    def head_cols(h):
        return slice(h * dk, (h + 1) * dk)

    def gate_col(quantity, h):
        lane = quantity * hb + h
        return gcol[:, lane:lane + 1]

    qn, kn, kb, rhs, decay, qd16, kd16, egl = [], [], [], [], [], [], [], []
    for h in heads:
        qh, kh, vh = qc[:, head_cols(h)], kc[:, head_cols(h)], vc[:, head_cols(h)]
        qn_h = qh * lax.rsqrt(jnp.sum(qh * qh, axis=-1, keepdims=True) + EPS) * (dk ** -0.5)
        kn_h = kh * lax.rsqrt(jnp.sum(kh * kh, axis=-1, keepdims=True) + EPS)
        beta = jnp.broadcast_to(gate_col(0, h), (c, dk))
        gc = jnp.broadcast_to(gate_col(1, h), (c, dk))
        gl = gc[c - 1:c, :]
        eg = jnp.exp(gc)
        ekd = jnp.exp(gl - gc)
        egl.append(jnp.exp(gl))
        gr = grow[hb + h:hb + h + 1, :]
        gc_cc = gc if c == dk else jnp.broadcast_to(gate_col(1, h), (c, c))
        decay.append(jnp.where(causal, jnp.exp(jnp.where(causal, gc_cc - gr, 0.0)), 0.0))
        kb_h = kn_h * beta
        qn.append(qn_h.astype(BF16))
        kn.append(kn_h.astype(BF16))
        kb.append(kb_h.astype(BF16))
        rhs.append(jnp.concatenate([(vh * beta).astype(BF16), (kb_h * eg).astype(BF16)], axis=1))
        qd16.append((qn_h * eg).astype(BF16))
        kd16.append((kn_h * ekd).astype(BF16))

    kk = [_dot_nt(kb[h], kn[h]) for h in heads]
    qk16 = [(_dot_nt(qn[h], kn[h]) * decay[h]).astype(BF16) for h in heads]
    lows = [jnp.where(strict, kk[h] * decay[h], 0.0) for h in heads]
    ainv = _unit_lower_inverses(lows, blk_mask, eye)
    uw = [_dot(ainv[h].astype(BF16), rhs[h]) for h in heads]
    s = [s_ref[h] for h in heads]
    s16 = [s[h].astype(BF16) for h in heads]
    ws_qs = [_dot(jnp.concatenate([uw[h][:, dk:].astype(BF16), qd16[h]], axis=0), s16[h]) for h in heads]
    vn16 = [(uw[h][:, :dk] - ws_qs[h][:c]).astype(BF16) for h in heads]
    o = [ws_qs[h][c:] + _dot(qk16[h], vn16[h]) for h in heads]
    for h in heads:
        s_ref[h] = s[h] * egl[h] + _dot_tn(kd16[h], vn16[h])
    for h in heads:
        on = o[h] * lax.rsqrt(jnp.mean(o[h] * o[h], axis=-1, keepdims=True) + EPS) * ng
        o_ref[:, head_cols(h)] = (on * _silu(z_ref[:, head_cols(h)])).astype(o_ref.dtype)


def _delta_rule(proj, conv_w, gcol, grow, dn_norm_g, bsz, seq, n_heads, dk, hb):
    t = proj.shape[0]
    c = DELTA_CHUNK
    nc = seq // c
    kconv = conv_w.shape[0]
    wblk = hb * dk
    n_hg = n_heads // hb
    width = n_heads * dk

    def tok(b, hg, n):
        return b * nc + n

    def proj_spec(part):
        return pl.BlockSpec((c, wblk), lambda b, hg, n: (tok(b, hg, n), part * n_hg + hg))

    def conv_spec(part):
        return pl.BlockSpec((kconv, wblk), lambda b, hg, n: (0, part * n_hg + hg))

    return pl.pallas_call(
        functools.partial(_delta_kernel, hb=hb, dk=dk, kconv=kconv),
        grid=(bsz, n_hg, nc),
        in_specs=[
            proj_spec(0), proj_spec(1), proj_spec(2), proj_spec(3),
            conv_spec(0), conv_spec(1), conv_spec(2),
            pl.BlockSpec((c, V7X_LANES), lambda b, hg, n: (tok(b, hg, n), hg)),
            pl.BlockSpec((V7X_LANES, c), lambda b, hg, n: (hg, tok(b, hg, n))),
            pl.BlockSpec((1, dk), lambda b, hg, n: (0, 0)),
        ],
        out_specs=pl.BlockSpec((c, wblk), lambda b, hg, n: (tok(b, hg, n), hg)),
        out_shape=jax.ShapeDtypeStruct((t, width), BF16),
        scratch_shapes=[
            pltpu.VMEM((V7X_SUBLANES, wblk), F32),
            pltpu.VMEM((V7X_SUBLANES, wblk), F32),
            pltpu.VMEM((V7X_SUBLANES, wblk), F32),
            pltpu.VMEM((hb, dk, dk), F32),
        ],
        compiler_params=_params(("parallel", "parallel", "arbitrary")),
        name="gated_delta_rule",
    )(proj, proj, proj, proj, conv_w, conv_w, conv_w, gcol, grow, dn_norm_g.reshape(1, dk))


def _pool_kernel(u_ref, w_ref, sc_ref, o_ref, buf, *, gw, hist):
    bt = u_ref.shape[0]
    n = pl.program_id(1)

    @pl.when(n == 0)
    def _():
        buf[0:hist, :] = jnp.zeros((hist, buf.shape[1]), F32)

    cur = u_ref[...]
    buf[hist:hist + bt, :] = cur
    pos = n * bt + lax.broadcasted_iota(jnp.int32, (bt, 1), 0)
    for g, window in enumerate(POOL_WINDOWS):
        cols = slice(g * gw, (g + 1) * gw)
        cg = cur[:, cols]
        acc = cg
        for d in range(1, window):
            acc = acc + buf[pl.ds(hist - d, bt), cols]
        count = jnp.minimum(pos + 1, window).astype(F32)
        pooled = acc / count - cg
        p = _dot(pooled.astype(BF16), w_ref[g])
        p = p * lax.rsqrt(jnp.mean(p * p, axis=-1, keepdims=True) + EPS)
        o_ref[:, cols] = (p * sc_ref[:, cols]).astype(o_ref.dtype)
    buf[0:hist, :] = cur[bt - hist:bt, :]


def _multiscale_pool(pool_in, pool_w16, pool_scale, bsz, seq):
    t = pool_in.shape[0]
    ng, gw, _ = pool_w16.shape
    width = ng * gw
    hist = max(POOL_WINDOWS)
    bt = min(256, seq)
    per_seq = seq // bt
    return pl.pallas_call(
        functools.partial(_pool_kernel, gw=gw, hist=hist),
        grid=(bsz, per_seq),
        in_specs=[
            pl.BlockSpec((bt, width), lambda b, n: (b * per_seq + n, 0)),
            pl.BlockSpec((ng, gw, gw), lambda b, n: (0, 0, 0)),
            pl.BlockSpec((1, width), lambda b, n: (0, 0)),
        ],
        out_specs=pl.BlockSpec((bt, width), lambda b, n: (b * per_seq + n, 0)),
        out_shape=jax.ShapeDtypeStruct((t, width), BF16),
        scratch_shapes=[pltpu.VMEM((bt + hist, width), F32)],
        compiler_params=_params(("parallel", "arbitrary")),
        name="multiscale_pool",
    )(pool_in, pool_w16, pool_scale.reshape(1, width))


def _heads_per_step(n_heads):
    hb = min(n_heads, 16)
    while n_heads % hb:
        hb -= 1
    return hb


def _hybrid_layer(x2, bsz, seq, c, w_ada, b_ada, norm1_g, w_in, conv_w, a_log, dt_bias, dn_norm_g,
                  pool_w, pool_scale, w_out, norm2_g, w_ff1, w_ff2):
    d = x2.shape[1]
    n_heads = a_log.shape[0]
    dk = dn_norm_g.shape[0]
    dn_width = n_heads * dk
    pool_width = pool_scale.shape[0]
    hb = _heads_per_step(n_heads)
    n_hg = n_heads // hb
    assert N_GATE_QUANTITIES * hb <= V7X_LANES
    assert pool_width == dn_width and seq % DELTA_CHUNK == 0

    mod = _modulation(c, w_ada, b_ada)
    sh1, sc1, gt1, sh2, sc2, gt2 = [m.reshape(bsz, 1, d) for m in jnp.split(mod, N_MOD, axis=-1)]

    o2 = 4 * dn_width
    o3 = o2 + n_heads
    o4 = o3 + n_heads
    w_pool = w_in[:, o4:]
    w_b = w_in[:, o2:o3].reshape(d, n_hg, hb)
    w_a = w_in[:, o3:o4].reshape(d, n_hg, hb)
    lane_pad = V7X_LANES - N_GATE_QUANTITIES * hb
    w_ba = jnp.concatenate([w_b] + [w_a] * (N_GATE_QUANTITIES - 1) + [jnp.zeros((d, n_hg, lane_pad), F32)],
                           axis=2).reshape(d, n_hg * V7X_LANES).astype(BF16)

    def lane_pack(v):
        vv = v.astype(F32).reshape(n_hg, hb)
        return jnp.concatenate([jnp.zeros((n_hg, hb), F32)] + [vv] * (N_GATE_QUANTITIES - 1)
                               + [jnp.zeros((n_hg, lane_pad), F32)], axis=1).reshape(1, n_hg * V7X_LANES)

    h1 = _norm_mod(x2, norm1_g, sc1, sh1, seq)
    proj = _matmul_ws(h1, w_in, o2, F32, False, "in_proj")
    pool_in = _matmul_ws(h1, w_pool, pool_width, F32, False, "pool_proj")
    ba = _matmul(h1, w_ba, F32, _mm_kernel, 1024, n_hg * V7X_LANES, "gate_proj")
    gcol, grow = _gates(ba, lane_pack(a_log), lane_pack(dt_bias), hb, DELTA_CHUNK)

    o = _delta_rule(proj, conv_w, gcol, grow, dn_norm_g, bsz, seq, n_heads, dk, hb)
    p = _multiscale_pool(pool_in, pool_w.astype(BF16), pool_scale, bsz, seq)

    x1 = _out_proj_residual(o, p, w_out, x2, gt1, seq)

    h2 = _norm_mod(x1, norm2_g, sc2, sh2, seq)
    a = _matmul_ws(h2, w_ff1, w_ff1.shape[1], BF16, True, "mlp_up_relu2")
    return _mlp_down_residual(a, w_ff2.astype(BF16), x1, gt2, seq)


def kernel(x, c, w_ada, b_ada, norm1_g, w_in, conv_w, a_log, dt_bias, dn_norm_g, pool_w, pool_scale,
           w_out, norm2_g, w_ff1, w_ff2, final_norm_g):
    bsz, seq, d = x.shape
    x2 = x.reshape(bsz * seq, d)
    for l in range(w_ada.shape[0]):
        x2 = _hybrid_layer(x2, bsz, seq, c, w_ada[l], b_ada[l], norm1_g[l], w_in[l], conv_w[l], a_log[l],
                           dt_bias[l], dn_norm_g[l], pool_w[l], pool_scale[l], w_out[l], norm2_g[l],
                           w_ff1[l], w_ff2[l])
    return _final_norm(x2, final_norm_g).reshape(bsz, seq, d)
```

```python
import functools

import jax
import jax.numpy as jnp
from jax import lax
from jax.experimental import pallas as pl
from jax.experimental.pallas import tpu as pltpu

F32 = jnp.float32
BF16 = jnp.bfloat16
EPS = 1e-6

V7X_LANES = 128
V7X_SUBLANES = 8
V7X_VMEM_LIMIT_BYTES = 60 * 1024 * 1024

POOL_WINDOWS = (2, 4, 8, 16)
DELTA_CHUNK = 128
SOLVE_BASE = 2 * V7X_SUBLANES
N_MOD = 6
N_GATE_QUANTITIES = 2


def _params(semantics):
    return pltpu.CompilerParams(dimension_semantics=semantics, vmem_limit_bytes=V7X_VMEM_LIMIT_BYTES)


def _silu(x):
    return x * jax.nn.sigmoid(x)


def _dot(a, b):
    return jnp.dot(a, b, preferred_element_type=F32)


def _dot_nt(a, b):
    return lax.dot_general(a, b, (((1,), (1,)), ((), ())), preferred_element_type=F32)


def _dot_tn(a, b):
    return lax.dot_general(a, b, (((0,), (0,)), ((), ())), preferred_element_type=F32)


def _mod_kernel(c_ref, w_ref, b_ref, o_ref):
    a = _silu(c_ref[...]).astype(BF16)
    o_ref[...] = _dot(a, w_ref[...].astype(BF16)) + b_ref[...]


def _modulation(c, w_ada, b_ada):
    bsz, d = c.shape
    n = w_ada.shape[1]
    rows = V7X_SUBLANES
    c_pad = jnp.zeros((rows, d), F32).at[:bsz].set(c)
    bn = min(512, n)
    out = pl.pallas_call(
        _mod_kernel,
        grid=(n // bn,),
        in_specs=[
            pl.BlockSpec((rows, d), lambda j: (0, 0)),
            pl.BlockSpec((d, bn), lambda j: (0, j)),
            pl.BlockSpec((1, bn), lambda j: (0, j)),
        ],
        out_specs=pl.BlockSpec((rows, bn), lambda j: (0, j)),
        out_shape=jax.ShapeDtypeStruct((rows, n), F32),
        compiler_params=_params(("parallel",)),
        name="adaln_modulation",
    )(c_pad, w_ada, b_ada.reshape(1, n))
    return out[:bsz]


def _norm_mod_kernel(x_ref, g_ref, sc_ref, sh_ref, o_ref):
    x = x_ref[...]
    y = x * lax.rsqrt(jnp.mean(x * x, axis=-1, keepdims=True) + EPS)
    y = y * g_ref[...]
    o_ref[...] = (y * (1.0 + sc_ref[0]) + sh_ref[0]).astype(o_ref.dtype)


def _norm_kernel(x_ref, g_ref, o_ref):
    x = x_ref[...]
    y = x * lax.rsqrt(jnp.mean(x * x, axis=-1, keepdims=True) + EPS)
    o_ref[...] = (y * g_ref[...]).astype(o_ref.dtype)


def _norm_mod(x2, g, sc, sh, seq):
    t, d = x2.shape
    bt = min(512, seq)
    per_seq = seq // bt
    return pl.pallas_call(
        _norm_mod_kernel,
        grid=(t // bt,),
        in_specs=[
            pl.BlockSpec((bt, d), lambda i: (i, 0)),
            pl.BlockSpec((1, d), lambda i: (0, 0)),
            pl.BlockSpec((1, 1, d), lambda i: (i // per_seq, 0, 0)),
            pl.BlockSpec((1, 1, d), lambda i: (i // per_seq, 0, 0)),
        ],
        out_specs=pl.BlockSpec((bt, d), lambda i: (i, 0)),
        out_shape=jax.ShapeDtypeStruct((t, d), BF16),
        compiler_params=_params(("parallel",)),
        name="rmsnorm_modulate",
    )(x2, g.reshape(1, d), sc, sh)


def _final_norm(x2, g):
    t, d = x2.shape
    bt = min(512, t)
    return pl.pallas_call(
        _norm_kernel,
        grid=(t // bt,),
        in_specs=[pl.BlockSpec((bt, d), lambda i: (i, 0)), pl.BlockSpec((1, d), lambda i: (0, 0))],
        out_specs=pl.BlockSpec((bt, d), lambda i: (i, 0)),
        out_shape=jax.ShapeDtypeStruct((t, d), F32),
        compiler_params=_params(("parallel",)),
        name="final_rmsnorm",
    )(x2, g.reshape(1, d))


def _fit(block, dim):
    block = min(block, dim)
    while dim % block:
        block -= V7X_LANES
    return block


def _stage_weight_chunk(wchunk_ref, wslots, n_col_blocks):
    p, i = pl.program_id(0), pl.program_id(1)
    rows = wchunk_ref.shape[0]

    @pl.when(p < n_col_blocks)
    def _():
        start = pl.multiple_of(i * rows, rows)
        wslots[p % 2, pl.ds(start, rows), :] = wchunk_ref[...].astype(BF16)


def _mm_ws_kernel(a_ref, wchunk_ref, *rest, n_col_blocks, relu2, w_transposed, has_side):
    if has_side:
        side_ref, o_ref, side_out_ref, wslots = rest
    else:
        o_ref, wslots = rest
    _stage_weight_chunk(wchunk_ref, wslots, n_col_blocks)
    p = pl.program_id(0)

    @pl.when(p > 0)
    def _():
        w = wslots[(p - 1) % 2]
        acc = _dot_nt(a_ref[...], w) if w_transposed else _dot(a_ref[...], w)
        if relu2:
            acc = jnp.maximum(acc, 0.0)
            acc = acc * acc
        o_ref[...] = acc.astype(o_ref.dtype)
        if has_side:
            side_out_ref[...] = side_ref[...].astype(side_out_ref.dtype)


def _mm_ws_out_kernel(o_ref, p_ref, wchunk_ref, x_ref, gt_ref, out_ref, wslots, *, n_col_blocks):
    _stage_weight_chunk(wchunk_ref, wslots, n_col_blocks)
    p = pl.program_id(0)
    ko = o_ref.shape[1]

    @pl.when(p > 0)
    def _():
        slot = (p - 1) % 2
        acc = _dot(o_ref[...], wslots[slot, 0:ko, :]) + _dot(p_ref[...], wslots[slot, ko:, :])
        out_ref[...] = x_ref[...] + gt_ref[0] * acc


def _ws_geometry(m, k, n, bm, bn):
    bm, bn = _fit(bm, m), _fit(bn, n)
    n_row_blocks = m // bm
    rows = k // n_row_blocks
    assert rows * n_row_blocks == k and rows % (2 * V7X_SUBLANES) == 0
    return bm, bn, n_row_blocks, n // bn, rows


def _ws_maps(n_col_blocks):
    def row_block(p, i):
        return jnp.where(p == 0, 0, i)

    def chunk(p, i):
        return (jnp.where(p == n_col_blocks, 0, i), jnp.minimum(p, n_col_blocks - 1))

    def col_block(p, i):
        return jnp.maximum(p - 1, 0)

    return row_block, chunk, col_block


def _matmul_ws(a, w, n, out_dtype, relu2, name, w_transposed=False, side=None, bm=1024, bn=1024):
    m, k = a.shape
    bm, bn, n_row_blocks, n_col_blocks, rows = _ws_geometry(m, k, n, bm, bn)
    row_block, chunk, col_block = _ws_maps(n_col_blocks)
    if w_transposed:
        rows = bn // n_row_blocks
        assert rows * n_row_blocks == bn and rows % (2 * V7X_SUBLANES) == 0
        w_spec = pl.BlockSpec(
            (rows, k),
            lambda p, i: (jnp.minimum(p, n_col_blocks - 1) * n_row_blocks + jnp.where(p == n_col_blocks, 0, i), 0))
        slots = pltpu.VMEM((2, bn, k), BF16)
    else:
        w_spec = pl.BlockSpec((rows, bn), chunk)
        slots = pltpu.VMEM((2, k, bn), BF16)
    in_specs = [pl.BlockSpec((bm, k), lambda p, i: (row_block(p, i), 0)), w_spec]
    out_specs = [pl.BlockSpec((bm, bn), lambda p, i: (row_block(p, i), col_block(p, i)))]
    out_shape = [jax.ShapeDtypeStruct((m, n), out_dtype)]
    operands = [a, w]
    if side is not None:
        side_rows, side_cols = side.shape
        slab = side_rows // (n_col_blocks * n_row_blocks)
        assert slab * n_col_blocks * n_row_blocks == side_rows and slab % (2 * V7X_SUBLANES) == 0
        side_spec = pl.BlockSpec(
            (slab, side_cols), lambda p, i: (jnp.where(p == 0, 0, (p - 1) * n_row_blocks + i), 0))
        in_specs.append(side_spec)
        out_specs.append(side_spec)
        out_shape.append(jax.ShapeDtypeStruct(side.shape, BF16))
        operands.append(side)
    outs = pl.pallas_call(
        functools.partial(_mm_ws_kernel, n_col_blocks=n_col_blocks, relu2=relu2, w_transposed=w_transposed,
                          has_side=side is not None),
        grid=(n_col_blocks + 1, n_row_blocks),
        in_specs=in_specs,
        out_specs=out_specs,
        out_shape=out_shape,
        scratch_shapes=[slots],
        compiler_params=_params(("arbitrary", "arbitrary")),
        name=name,
    )(*operands)
    return outs[0] if side is None else tuple(outs)


def _out_proj_residual(o, p, w_out, x2, gt, seq, bm=1024, bn=1024):
    t, ko = o.shape
    k, d = w_out.shape
    assert ko + p.shape[1] == k
    bm, bn, n_row_blocks, n_col_blocks, rows = _ws_geometry(t, k, d, min(bm, seq), bn)
    per_seq = seq // bm
    row_block, chunk, col_block = _ws_maps(n_col_blocks)
    return pl.pallas_call(
        functools.partial(_mm_ws_out_kernel, n_col_blocks=n_col_blocks),
        grid=(n_col_blocks + 1, n_row_blocks),
        in_specs=[
            pl.BlockSpec((bm, ko), lambda p, i: (row_block(p, i), 0)),
            pl.BlockSpec((bm, k - ko), lambda p, i: (row_block(p, i), 0)),
            pl.BlockSpec((rows, bn), chunk),
            pl.BlockSpec((bm, bn), lambda p, i: (row_block(p, i), col_block(p, i))),
            pl.BlockSpec((1, 1, bn), lambda p, i: (row_block(p, i) // per_seq, 0, col_block(p, i))),
        ],
        out_specs=pl.BlockSpec((bm, bn), lambda p, i: (row_block(p, i), col_block(p, i))),
        out_shape=jax.ShapeDtypeStruct((t, d), F32),
        scratch_shapes=[pltpu.VMEM((2, k, bn), BF16)],
        compiler_params=_params(("arbitrary", "arbitrary")),
        name="out_proj_residual",
    )(o, p, w_out, x2, gt)


def _mm_down_kernel(a_ref, w_ref, x_ref, gt_ref, o_ref, *, nk):
    kk = pl.program_id(2)
    part = _dot(a_ref[...], w_ref[...])
    if nk == 1:
        o_ref[...] = x_ref[...] + gt_ref[0] * part
        return

    @pl.when(kk == 0)
    def _():
        o_ref[...] = part

    @pl.when(jnp.logical_and(kk > 0, kk < nk - 1))
    def _():
        o_ref[...] += part

    @pl.when(kk == nk - 1)
    def _():
        o_ref[...] = x_ref[...] + gt_ref[0] * (o_ref[...] + part)


def _mlp_down_residual(a, w, x2, gt, seq, bm=1024, bn=1024, bk=4096):
    t, k = a.shape
    d = w.shape[1]
    bm, bn, bk = min(bm, seq), min(bn, d), min(bk, k)
    per_seq = seq // bm
    nk = k // bk
    return pl.pallas_call(
        functools.partial(_mm_down_kernel, nk=nk),
        grid=(t // bm, d // bn, nk),
        in_specs=[
            pl.BlockSpec((bm, bk), lambda i, j, kk: (i, kk)),
            pl.BlockSpec((bk, bn), lambda i, j, kk: (kk, j)),
            pl.BlockSpec((bm, bn), lambda i, j, kk: (i, j)),
            pl.BlockSpec((1, 1, bn), lambda i, j, kk: (i // per_seq, 0, j)),
        ],
        out_specs=pl.BlockSpec((bm, bn), lambda i, j, kk: (i, j)),
        out_shape=jax.ShapeDtypeStruct((t, d), F32),
        compiler_params=_params(("parallel", "parallel", "arbitrary")),
        name="mlp_down_residual",
    )(a, w, x2, gt)


def _split3(x):
    hi = x.astype(BF16)
    r = x - hi.astype(F32)
    mid = r.astype(BF16)
    lo = (r - mid.astype(F32)).astype(BF16)
    return hi, mid, lo


def _gate_kernel(ba_ref, alog_ref, dtb_ref, col_ref, row_ref, *, hb, chunk):
    bt, lanes = ba_ref.shape
    x = ba_ref[...]
    beta = jax.nn.sigmoid(x)
    z = x + dtb_ref[...]
    softplus = jnp.maximum(z, 0.0) + jnp.log1p(jnp.exp(-jnp.abs(z)))
    g = -jnp.exp(alog_ref[...]) * softplus

    ri = lax.broadcasted_iota(jnp.int32, (chunk, chunk), 0)
    ci = lax.broadcasted_iota(jnp.int32, (chunk, chunk), 1)
    tril = (ri >= ci).astype(BF16)
    quantity = lax.broadcasted_iota(jnp.int32, (chunk, lanes), 1) // hb

    for c in range(bt // chunk):
        rows = slice(c * chunk, (c + 1) * chunk)
        hi, mid, lo = _split3(g[rows])
        gc = _dot(tril, hi) + _dot(tril, mid) + _dot(tril, lo)
        out = jnp.where(quantity == 0, beta[rows], jnp.where(quantity == 1, gc, 0.0))
        col_ref[rows, :] = out
        row_ref[:, rows] = out.T


def _gates(ba, alog_l, dtb_l, hb, chunk):
    t, width = ba.shape
    n_hg = width // V7X_LANES
    bt = min(1024, t)
    return pl.pallas_call(
        functools.partial(_gate_kernel, hb=hb, chunk=chunk),
        grid=(n_hg, t // bt),
        in_specs=[
            pl.BlockSpec((bt, V7X_LANES), lambda h, i: (i, h)),
            pl.BlockSpec((1, V7X_LANES), lambda h, i: (0, h)),
            pl.BlockSpec((1, V7X_LANES), lambda h, i: (0, h)),
        ],
        out_specs=[
            pl.BlockSpec((bt, V7X_LANES), lambda h, i: (i, h)),
            pl.BlockSpec((V7X_LANES, bt), lambda h, i: (h, i)),
        ],
        out_shape=[
            jax.ShapeDtypeStruct((t, width), F32),
            jax.ShapeDtypeStruct((width, t), F32),
        ],
        compiler_params=_params(("parallel", "parallel")),
        name="delta_gates",
    )(ba, alog_l, dtb_l)


def _unit_lower_inverses(lows, blk_mask, eye):
    c = lows[0].shape[0]
    nblk = c // SOLVE_BASE
    diags = [jnp.where(blk_mask, low, 0.0) for low in lows]
    rests = [(low - diag).astype(BF16) for low, diag in zip(lows, diags)]

    sub = lax.broadcasted_iota(jnp.int32, (SOLVE_BASE, c), 0)
    lane = lax.broadcasted_iota(jnp.int32, (SOLVE_BASE, c), 1)
    blk_start = (lane // SOLVE_BASE) * SOLVE_BASE
    compact_eye = (sub == lane - blk_start).astype(F32)

    def block_inverse(diag):
        lc = jnp.sum(diag.reshape(nblk, SOLVE_BASE, c), axis=0)
        x = compact_eye
        for j in range(SOLVE_BASE - 1):
            mult = jnp.take_along_axis(lc, blk_start + j, axis=1)
            x = x - mult * jnp.broadcast_to(x[j:j + 1, :], (SOLVE_BASE, c))
        return jnp.where(blk_mask, jnp.tile(x, (nblk, 1)), 0.0)

    invs = [block_inverse(diag) for diag in diags]
    zps = [_dot(inv.astype(BF16), rest).astype(BF16) for inv, rest in zip(invs, rests)]
    invs = [inv - _dot(zp, inv.astype(BF16)) for zp, inv in zip(zps, invs)]
    span = 2
    while span < nblk:
        zps = [_dot(zp, zp).astype(BF16) for zp in zps]
        invs = [inv + _dot(zp, inv.astype(BF16)) for zp, inv in zip(zps, invs)]
        span *= 2
    return invs


def _delta_kernel(q_ref, k_ref, v_ref, z_ref, cwq_ref, cwk_ref, cwv_ref, gcol_ref, grow_ref, ng_ref,
                  o_ref, qbuf, kbuf, vbuf, s_ref, *, hb, dk, kconv):
    c = q_ref.shape[0]
    pad = V7X_SUBLANES
    first = pl.program_id(2) == 0

    @pl.when(first)
    def _():
        for tail in (qbuf, kbuf, vbuf):
            tail[...] = jnp.zeros_like(tail)
        s_ref[...] = jnp.zeros_like(s_ref)

    def conv_silu(x_ref, tail_ref, cw_ref):
        cur = x_ref[...]
        width = cur.shape[1]
        tiles = jnp.concatenate([tail_ref[...], cur], axis=0).reshape(c // pad + 1, pad, width)
        sublane = lax.broadcasted_iota(jnp.int32, (1, pad, width), 1)
        cw = cw_ref[...]
        y = cur * cw[kconv - 1:kconv, :]
        for s in range(1, kconv):
            rot = pltpu.roll(tiles, s, 1)
            shifted = jnp.where(sublane < s, rot[:-1], rot[1:]).reshape(c, width)
            y = y + shifted * cw[kconv - 1 - s:kconv - s, :]
        tail_ref[...] = cur[c - pad:c, :]
        return _silu(y)

    qc = conv_silu(q_ref, qbuf, cwq_ref)
    kc = conv_silu(k_ref, kbuf, cwk_ref)
    vc = conv_silu(v_ref, vbuf, cwv_ref)

    ri = lax.broadcasted_iota(jnp.int32, (c, c), 0)
    ci = lax.broadcasted_iota(jnp.int32, (c, c), 1)
    causal = ri >= ci
    strict = ri > ci
    blk_mask = (ri // SOLVE_BASE) == (ci // SOLVE_BASE)
    eye = (ri == ci).astype(F32)
    gcol = gcol_ref[...]
    grow = grow_ref[...]
    ng = ng_ref[...]

    heads = range(hb)

    def head_cols(h):
        return slice(h * dk, (h + 1) * dk)

    def gate_col(quantity, h):
        lane = quantity * hb + h
        return gcol[:, lane:lane + 1]

    qn, kn, kb, rhs, decay, qd16, kd16, egl = [], [], [], [], [], [], [], []
    for h in heads:
        qh, kh, vh = qc[:, head_cols(h)], kc[:, head_cols(h)], vc[:, head_cols(h)]
        qn_h = qh * lax.rsqrt(jnp.sum(qh * qh, axis=-1, keepdims=True) + EPS) * (dk ** -0.5)
        kn_h = kh * lax.rsqrt(jnp.sum(kh * kh, axis=-1, keepdims=True) + EPS)
        beta = jnp.broadcast_to(gate_col(0, h), (c, dk))
        gc = jnp.broadcast_to(gate_col(1, h), (c, dk))
        gl = gc[c - 1:c, :]
        eg = jnp.exp(gc)
        ekd = jnp.exp(gl - gc)
        egl.append(jnp.exp(gl))
        gr = grow[hb + h:hb + h + 1, :]
        gc_cc = gc if c == dk else jnp.broadcast_to(gate_col(1, h), (c, c))
        decay.append(jnp.where(causal, jnp.exp(jnp.where(causal, gc_cc - gr, 0.0)), 0.0))
        kb_h = kn_h * beta
        qn.append(qn_h.astype(BF16))
        kn.append(kn_h.astype(BF16))
        kb.append(kb_h.astype(BF16))
        rhs.append(jnp.concatenate([(vh * beta).astype(BF16), (kb_h * eg).astype(BF16)], axis=1))
        qd16.append((qn_h * eg).astype(BF16))
        kd16.append((kn_h * ekd).astype(BF16))

    kk = [_dot_nt(kb[h], kn[h]) for h in heads]
    qk16 = [(_dot_nt(qn[h], kn[h]) * decay[h]).astype(BF16) for h in heads]
    lows = [jnp.where(strict, kk[h] * decay[h], 0.0) for h in heads]
    ainv = _unit_lower_inverses(lows, blk_mask, eye)
    uw = [_dot(ainv[h].astype(BF16), rhs[h]) for h in heads]
    s = [s_ref[h] for h in heads]
    s16 = [s[h].astype(BF16) for h in heads]
    ws_qs = [_dot(jnp.concatenate([uw[h][:, dk:].astype(BF16), qd16[h]], axis=0), s16[h]) for h in heads]
    vn16 = [(uw[h][:, :dk] - ws_qs[h][:c]).astype(BF16) for h in heads]
    o = [ws_qs[h][c:] + _dot(qk16[h], vn16[h]) for h in heads]
    for h in heads:
        s_ref[h] = s[h] * egl[h] + _dot_tn(kd16[h], vn16[h])
    for h in heads:
        on = o[h] * lax.rsqrt(jnp.mean(o[h] * o[h], axis=-1, keepdims=True) + EPS) * ng
        o_ref[:, head_cols(h)] = (on * _silu(z_ref[:, head_cols(h)])).astype(o_ref.dtype)


def _delta_rule(proj, conv_w, gcol, grow, dn_norm_g, bsz, seq, n_heads, dk, hb):
    t = proj.shape[0]
    c = DELTA_CHUNK
    nc = seq // c
    kconv = conv_w.shape[0]
    wblk = hb * dk
    n_hg = n_heads // hb
    width = n_heads * dk

    def tok(b, hg, n):
        return b * nc + n

    def proj_spec(part):
        return pl.BlockSpec((c, wblk), lambda b, hg, n: (tok(b, hg, n), part * n_hg + hg))

    def conv_spec(part):
        return pl.BlockSpec((kconv, wblk), lambda b, hg, n: (0, part * n_hg + hg))

    return pl.pallas_call(
        functools.partial(_delta_kernel, hb=hb, dk=dk, kconv=kconv),
        grid=(bsz, n_hg, nc),
        in_specs=[
            proj_spec(0), proj_spec(1), proj_spec(2), proj_spec(3),
            conv_spec(0), conv_spec(1), conv_spec(2),
            pl.BlockSpec((c, V7X_LANES), lambda b, hg, n: (tok(b, hg, n), hg)),
            pl.BlockSpec((V7X_LANES, c), lambda b, hg, n: (hg, tok(b, hg, n))),
            pl.BlockSpec((1, dk), lambda b, hg, n: (0, 0)),
        ],
        out_specs=pl.BlockSpec((c, wblk), lambda b, hg, n: (tok(b, hg, n), hg)),
        out_shape=jax.ShapeDtypeStruct((t, width), BF16),
        scratch_shapes=[
            pltpu.VMEM((V7X_SUBLANES, wblk), F32),
            pltpu.VMEM((V7X_SUBLANES, wblk), F32),
            pltpu.VMEM((V7X_SUBLANES, wblk), F32),
            pltpu.VMEM((hb, dk, dk), F32),
        ],
        compiler_params=_params(("parallel", "parallel", "arbitrary")),
        name="gated_delta_rule",
    )(proj, proj, proj, proj, conv_w, conv_w, conv_w, gcol, grow, dn_norm_g.reshape(1, dk))


def _pool_kernel(u_ref, w_ref, sc_ref, o_ref, buf, *, gw, hist):
    bt = u_ref.shape[0]
    n = pl.program_id(1)

    @pl.when(n == 0)
    def _():
        buf[0:hist, :] = jnp.zeros((hist, buf.shape[1]), F32)

    cur = u_ref[...]
    buf[hist:hist + bt, :] = cur
    pos = n * bt + lax.broadcasted_iota(jnp.int32, (bt, 1), 0)
    for g, window in enumerate(POOL_WINDOWS):
        cols = slice(g * gw, (g + 1) * gw)
        cg = cur[:, cols]
        acc = cg
        for d in range(1, window):
            acc = acc + buf[pl.ds(hist - d, bt), cols]
        count = jnp.minimum(pos + 1, window).astype(F32)
        pooled = acc / count - cg
        p = _dot(pooled.astype(BF16), w_ref[g])
        p = p * lax.rsqrt(jnp.mean(p * p, axis=-1, keepdims=True) + EPS)
        o_ref[:, cols] = (p * sc_ref[:, cols]).astype(o_ref.dtype)
    buf[0:hist, :] = cur[bt - hist:bt, :]


def _multiscale_pool(pool_in, pool_w16, pool_scale, bsz, seq):
    t = pool_in.shape[0]
    ng, gw, _ = pool_w16.shape
    width = ng * gw
    hist = max(POOL_WINDOWS)
    bt = min(256, seq)
    per_seq = seq // bt
    return pl.pallas_call(
        functools.partial(_pool_kernel, gw=gw, hist=hist),
        grid=(bsz, per_seq),
        in_specs=[
            pl.BlockSpec((bt, width), lambda b, n: (b * per_seq + n, 0)),
            pl.BlockSpec((ng, gw, gw), lambda b, n: (0, 0, 0)),
            pl.BlockSpec((1, width), lambda b, n: (0, 0)),
        ],
        out_specs=pl.BlockSpec((bt, width), lambda b, n: (b * per_seq + n, 0)),
        out_shape=jax.ShapeDtypeStruct((t, width), BF16),
        scratch_shapes=[pltpu.VMEM((bt + hist, width), F32)],
        compiler_params=_params(("parallel", "arbitrary")),
        name="multiscale_pool",
    )(pool_in, pool_w16, pool_scale.reshape(1, width))


def _heads_per_step(n_heads):
    hb = min(n_heads, 16)
    while n_heads % hb:
        hb -= 1
    return hb


def _hybrid_layer(x2, bsz, seq, c, w_ada, b_ada, norm1_g, w_in, conv_w, a_log, dt_bias, dn_norm_g,
                  pool_w, pool_scale, w_out, norm2_g, w_ff1, w_ff2):
    d = x2.shape[1]
    n_heads = a_log.shape[0]
    dk = dn_norm_g.shape[0]
    dn_width = n_heads * dk
    pool_width = pool_scale.shape[0]
    hb = _heads_per_step(n_heads)
    n_hg = n_heads // hb
    assert N_GATE_QUANTITIES * hb <= V7X_LANES
    assert pool_width == dn_width and seq % DELTA_CHUNK == 0

    mod = _modulation(c, w_ada, b_ada)
    sh1, sc1, gt1, sh2, sc2, gt2 = [m.reshape(bsz, 1, d) for m in jnp.split(mod, N_MOD, axis=-1)]

    o2 = 4 * dn_width
    o3 = o2 + n_heads
    o4 = o3 + n_heads
    w_in_t = w_in.T
    w_pool_t = w_in_t[o4:]
    w_b = w_in[:, o2:o3].reshape(d, n_hg, hb)
    w_a = w_in[:, o3:o4].reshape(d, n_hg, hb)
    lane_pad = V7X_LANES - N_GATE_QUANTITIES * hb
    w_ba = jnp.concatenate([w_b] + [w_a] * (N_GATE_QUANTITIES - 1) + [jnp.zeros((d, n_hg, lane_pad), F32)],
                           axis=2).reshape(d, n_hg * V7X_LANES)

    def lane_pack(v):
        vv = v.astype(F32).reshape(n_hg, hb)
        return jnp.concatenate([jnp.zeros((n_hg, hb), F32)] + [vv] * (N_GATE_QUANTITIES - 1)
                               + [jnp.zeros((n_hg, lane_pad), F32)], axis=1).reshape(1, n_hg * V7X_LANES)

    h1 = _norm_mod(x2, norm1_g, sc1, sh1, seq)
    proj = _matmul_ws(h1, w_in_t, o2, F32, False, "in_proj", w_transposed=True)
    pool_in = _matmul_ws(h1, w_pool_t, pool_width, F32, False, "pool_proj", w_transposed=True)
    ba = _matmul_ws(h1, w_ba, n_hg * V7X_LANES, F32, False, "gate_proj")
    gcol, grow = _gates(ba, lane_pack(a_log), lane_pack(dt_bias), hb, DELTA_CHUNK)

    o = _delta_rule(proj, conv_w, gcol, grow, dn_norm_g, bsz, seq, n_heads, dk, hb)
    p = _multiscale_pool(pool_in, pool_w.astype(BF16), pool_scale, bsz, seq)

    x1 = _out_proj_residual(o, p, w_out, x2, gt1, seq)

    h2 = _norm_mod(x1, norm2_g, sc2, sh2, seq)
    a, w_ff2_16 = _matmul_ws(h2, w_ff1, w_ff1.shape[1], BF16, True, "mlp_up_relu2", side=w_ff2)
    return _mlp_down_residual(a, w_ff2_16, x1, gt2, seq)


def kernel(x, c, w_ada, b_ada, norm1_g, w_in, conv_w, a_log, dt_bias, dn_norm_g, pool_w, pool_scale,
           w_out, norm2_g, w_ff1, w_ff2, final_norm_g):
    bsz, seq, d = x.shape
    x2 = x.reshape(bsz * seq, d)
    for l in range(w_ada.shape[0]):
        x2 = _hybrid_layer(x2, bsz, seq, c, w_ada[l], b_ada[l], norm1_g[l], w_in[l], conv_w[l], a_log[l],
                           dt_bias[l], dn_norm_g[l], pool_w[l], pool_scale[l], w_out[l], norm2_g[l],
                           w_ff1[l], w_ff2[l])
    return _final_norm(x2, final_norm_g).reshape(bsz, seq, d)
```

```python
import functools

import jax
import jax.numpy as jnp
from jax import lax
from jax.experimental import pallas as pl
from jax.experimental.pallas import tpu as pltpu

F32 = jnp.float32
BF16 = jnp.bfloat16
EPS = 1e-6

V7X_LANES = 128
V7X_SUBLANES = 8
V7X_VMEM_LIMIT_BYTES = 60 * 1024 * 1024

POOL_WINDOWS = (2, 4, 8, 16)
DELTA_CHUNK = 128
SOLVE_BASE = 2 * V7X_SUBLANES
N_MOD = 6
N_GATE_QUANTITIES = 2


def _params(semantics):
    return pltpu.CompilerParams(dimension_semantics=semantics, vmem_limit_bytes=V7X_VMEM_LIMIT_BYTES)


def _silu(x):
    return x * jax.nn.sigmoid(x)


def _dot(a, b):
    return jnp.dot(a, b, preferred_element_type=F32)


def _dot_nt(a, b):
    return lax.dot_general(a, b, (((1,), (1,)), ((), ())), preferred_element_type=F32)


def _dot_tn(a, b):
    return lax.dot_general(a, b, (((0,), (0,)), ((), ())), preferred_element_type=F32)


def _mod_kernel(c_ref, w_ref, b_ref, o_ref):
    a = _silu(c_ref[...]).astype(BF16)
    o_ref[...] = _dot(a, w_ref[...].astype(BF16)) + b_ref[...]


def _modulation(c, w_ada, b_ada):
    bsz, d = c.shape
    n = w_ada.shape[1]
    rows = V7X_SUBLANES
    c_pad = jnp.zeros((rows, d), F32).at[:bsz].set(c)
    bn = min(512, n)
    out = pl.pallas_call(
        _mod_kernel,
        grid=(n // bn,),
        in_specs=[
            pl.BlockSpec((rows, d), lambda j: (0, 0)),
            pl.BlockSpec((d, bn), lambda j: (0, j)),
            pl.BlockSpec((1, bn), lambda j: (0, j)),
        ],
        out_specs=pl.BlockSpec((rows, bn), lambda j: (0, j)),
        out_shape=jax.ShapeDtypeStruct((rows, n), F32),
        compiler_params=_params(("parallel",)),
        name="adaln_modulation",
    )(c_pad, w_ada, b_ada.reshape(1, n))
    return out[:bsz]


def _norm_mod_kernel(x_ref, g_ref, sc_ref, sh_ref, o_ref):
    x = x_ref[...]
    y = x * lax.rsqrt(jnp.mean(x * x, axis=-1, keepdims=True) + EPS)
    y = y * g_ref[...]
    o_ref[...] = (y * (1.0 + sc_ref[0]) + sh_ref[0]).astype(o_ref.dtype)


def _norm_kernel(x_ref, g_ref, o_ref):
    x = x_ref[...]
    y = x * lax.rsqrt(jnp.mean(x * x, axis=-1, keepdims=True) + EPS)
    o_ref[...] = (y * g_ref[...]).astype(o_ref.dtype)


def _norm_mod(x2, g, sc, sh, seq):
    t, d = x2.shape
    bt = min(512, seq)
    per_seq = seq // bt
    return pl.pallas_call(
        _norm_mod_kernel,
        grid=(t // bt,),
        in_specs=[
            pl.BlockSpec((bt, d), lambda i: (i, 0)),
            pl.BlockSpec((1, d), lambda i: (0, 0)),
            pl.BlockSpec((1, 1, d), lambda i: (i // per_seq, 0, 0)),
            pl.BlockSpec((1, 1, d), lambda i: (i // per_seq, 0, 0)),
        ],
        out_specs=pl.BlockSpec((bt, d), lambda i: (i, 0)),
        out_shape=jax.ShapeDtypeStruct((t, d), BF16),
        compiler_params=_params(("parallel",)),
        name="rmsnorm_modulate",
    )(x2, g.reshape(1, d), sc, sh)


def _final_norm(x2, g):
    t, d = x2.shape
    bt = min(512, t)
    return pl.pallas_call(
        _norm_kernel,
        grid=(t // bt,),
        in_specs=[pl.BlockSpec((bt, d), lambda i: (i, 0)), pl.BlockSpec((1, d), lambda i: (0, 0))],
        out_specs=pl.BlockSpec((bt, d), lambda i: (i, 0)),
        out_shape=jax.ShapeDtypeStruct((t, d), F32),
        compiler_params=_params(("parallel",)),
        name="final_rmsnorm",
    )(x2, g.reshape(1, d))


def _fit(block, dim):
    block = min(block, dim)
    while dim % block:
        block -= V7X_LANES
    return block


def _stage_weight_chunk(wchunk_ref, wslots, n_col_blocks):
    p, i = pl.program_id(0), pl.program_id(1)
    rows = wchunk_ref.shape[0]

    @pl.when(p < n_col_blocks)
    def _():
        start = pl.multiple_of(i * rows, rows)
        wslots[p % 2, pl.ds(start, rows), :] = wchunk_ref[...].astype(BF16)


def _mm_ws_kernel(a_ref, wchunk_ref, *rest, n_col_blocks, relu2, w_transposed, has_side):
    if has_side:
        side_ref, o_ref, side_out_ref, wslots = rest
    else:
        o_ref, wslots = rest
    _stage_weight_chunk(wchunk_ref, wslots, n_col_blocks)
    p = pl.program_id(0)

    @pl.when(p > 0)
    def _():
        w = wslots[(p - 1) % 2]
        acc = _dot_nt(a_ref[...], w) if w_transposed else _dot(a_ref[...], w)
        if relu2:
            acc = jnp.maximum(acc, 0.0)
            acc = acc * acc
        o_ref[...] = acc.astype(o_ref.dtype)
        if has_side:
            side_out_ref[...] = side_ref[...].astype(side_out_ref.dtype)


def _mm_ws_out_kernel(o_ref, p_ref, wchunk_ref, x_ref, gt_ref, out_ref, wslots, *, n_col_blocks):
    _stage_weight_chunk(wchunk_ref, wslots, n_col_blocks)
    p = pl.program_id(0)
    ko = o_ref.shape[1]

    @pl.when(p > 0)
    def _():
        slot = (p - 1) % 2
        acc = _dot(o_ref[...], wslots[slot, 0:ko, :]) + _dot(p_ref[...], wslots[slot, ko:, :])
        out_ref[...] = x_ref[...] + gt_ref[0] * acc


def _ws_geometry(m, k, n, bm, bn):
    bm, bn = _fit(bm, m), _fit(bn, n)
    n_row_blocks = m // bm
    rows = k // n_row_blocks
    assert rows * n_row_blocks == k and rows % (2 * V7X_SUBLANES) == 0
    return bm, bn, n_row_blocks, n // bn, rows


def _ws_maps(n_col_blocks):
    def row_block(p, i):
        return jnp.where(p == 0, 0, i)

    def chunk(p, i):
        return (jnp.where(p == n_col_blocks, 0, i), jnp.minimum(p, n_col_blocks - 1))

    def col_block(p, i):
        return jnp.maximum(p - 1, 0)

    return row_block, chunk, col_block


def _matmul_ws(a, w, n, out_dtype, relu2, name, w_transposed=False, side=None, bm=1024, bn=1024):
    m, k = a.shape
    bm, bn, n_row_blocks, n_col_blocks, rows = _ws_geometry(m, k, n, bm, bn)
    row_block, chunk, col_block = _ws_maps(n_col_blocks)
    if w_transposed:
        rows = bn // n_row_blocks
        assert rows * n_row_blocks == bn and rows % (2 * V7X_SUBLANES) == 0
        w_spec = pl.BlockSpec(
            (rows, k),
            lambda p, i: (jnp.minimum(p, n_col_blocks - 1) * n_row_blocks + jnp.where(p == n_col_blocks, 0, i), 0))
        slots = pltpu.VMEM((2, bn, k), BF16)
    else:
        w_spec = pl.BlockSpec((rows, bn), chunk)
        slots = pltpu.VMEM((2, k, bn), BF16)
    in_specs = [pl.BlockSpec((bm, k), lambda p, i: (row_block(p, i), 0)), w_spec]
    out_specs = [pl.BlockSpec((bm, bn), lambda p, i: (row_block(p, i), col_block(p, i)))]
    out_shape = [jax.ShapeDtypeStruct((m, n), out_dtype)]
    operands = [a, w]
    if side is not None:
        side_rows, side_cols = side.shape
        slab = side_rows // (n_col_blocks * n_row_blocks)
        assert slab * n_col_blocks * n_row_blocks == side_rows and slab % (2 * V7X_SUBLANES) == 0
        side_spec = pl.BlockSpec(
            (slab, side_cols), lambda p, i: (jnp.where(p == 0, 0, (p - 1) * n_row_blocks + i), 0))
        in_specs.append(side_spec)
        out_specs.append(side_spec)
        out_shape.append(jax.ShapeDtypeStruct(side.shape, BF16))
        operands.append(side)
    outs = pl.pallas_call(
        functools.partial(_mm_ws_kernel, n_col_blocks=n_col_blocks, relu2=relu2, w_transposed=w_transposed,
                          has_side=side is not None),
        grid=(n_col_blocks + 1, n_row_blocks),
        in_specs=in_specs,
        out_specs=out_specs,
        out_shape=out_shape,
        scratch_shapes=[slots],
        compiler_params=_params(("arbitrary", "arbitrary")),
        name=name,
    )(*operands)
    return outs[0] if side is None else tuple(outs)


def _out_proj_residual(o, p, w_out, x2, gt, seq, bm=1024, bn=1024):
    t, ko = o.shape
    k, d = w_out.shape
    assert ko + p.shape[1] == k
    bm, bn, n_row_blocks, n_col_blocks, rows = _ws_geometry(t, k, d, min(bm, seq), bn)
    per_seq = seq // bm
    row_block, chunk, col_block = _ws_maps(n_col_blocks)
    return pl.pallas_call(
        functools.partial(_mm_ws_out_kernel, n_col_blocks=n_col_blocks),
        grid=(n_col_blocks + 1, n_row_blocks),
        in_specs=[
            pl.BlockSpec((bm, ko), lambda p, i: (row_block(p, i), 0)),
            pl.BlockSpec((bm, k - ko), lambda p, i: (row_block(p, i), 0)),
            pl.BlockSpec((rows, bn), chunk),
            pl.BlockSpec((bm, bn), lambda p, i: (row_block(p, i), col_block(p, i))),
            pl.BlockSpec((1, 1, bn), lambda p, i: (row_block(p, i) // per_seq, 0, col_block(p, i))),
        ],
        out_specs=pl.BlockSpec((bm, bn), lambda p, i: (row_block(p, i), col_block(p, i))),
        out_shape=jax.ShapeDtypeStruct((t, d), F32),
        scratch_shapes=[pltpu.VMEM((2, k, bn), BF16)],
        compiler_params=_params(("arbitrary", "arbitrary")),
        name="out_proj_residual",
    )(o, p, w_out, x2, gt)


def _mm_down_kernel(a_ref, w_ref, x_ref, gt_ref, o_ref, *, nk):
    kk = pl.program_id(2)
    if nk == 1:
        o_ref[...] = x_ref[...] + gt_ref[0] * _dot(a_ref[...], w_ref[...])
        return

    @pl.when(kk == 0)
    def _():
        o_ref[...] = _dot(a_ref[...], w_ref[...])

    @pl.when(jnp.logical_and(kk > 0, kk < nk - 1))
    def _():
        o_ref[...] += _dot(a_ref[...], w_ref[...])

    @pl.when(kk == nk - 1)
    def _():
        o_ref[...] = x_ref[...] + gt_ref[0] * (o_ref[...] + _dot(a_ref[...], w_ref[...]))


def _mlp_down_residual(a, w, x2, gt, seq, bm=1024, bn=1024, bk=4096):
    t, k = a.shape
    d = w.shape[1]
    bm, bn, bk = min(bm, seq), min(bn, d), min(bk, k)
    per_seq = seq // bm
    nk = k // bk
    return pl.pallas_call(
        functools.partial(_mm_down_kernel, nk=nk),
        grid=(t // bm, d // bn, nk),
        in_specs=[
            pl.BlockSpec((bm, bk), lambda i, j, kk: (i, kk)),
            pl.BlockSpec((bk, bn), lambda i, j, kk: (kk, j)),
            pl.BlockSpec((bm, bn), lambda i, j, kk: (i, j)),
            pl.BlockSpec((1, 1, bn), lambda i, j, kk: (i // per_seq, 0, j)),
        ],
        out_specs=pl.BlockSpec((bm, bn), lambda i, j, kk: (i, j)),
        out_shape=jax.ShapeDtypeStruct((t, d), F32),
        compiler_params=_params(("parallel", "parallel", "arbitrary")),
        name="mlp_down_residual",
    )(a, w, x2, gt)


def _split3(x):
    hi = x.astype(BF16)
    r = x - hi.astype(F32)
    mid = r.astype(BF16)
    lo = (r - mid.astype(F32)).astype(BF16)
    return hi, mid, lo


def _gate_kernel(ba_ref, alog_ref, dtb_ref, col_ref, row_ref, *, hb, chunk):
    bt, lanes = ba_ref.shape
    x = ba_ref[...]
    beta = jax.nn.sigmoid(x)
    z = x + dtb_ref[...]
    softplus = jnp.maximum(z, 0.0) + jnp.log1p(jnp.exp(-jnp.abs(z)))
    g = -jnp.exp(alog_ref[...]) * softplus

    ri = lax.broadcasted_iota(jnp.int32, (chunk, chunk), 0)
    ci = lax.broadcasted_iota(jnp.int32, (chunk, chunk), 1)
    tril = (ri >= ci).astype(BF16)
    quantity = lax.broadcasted_iota(jnp.int32, (chunk, lanes), 1) // hb

    for c in range(bt // chunk):
        rows = slice(c * chunk, (c + 1) * chunk)
        hi, mid, lo = _split3(g[rows])
        gc = _dot(tril, hi) + _dot(tril, mid) + _dot(tril, lo)
        out = jnp.where(quantity == 0, beta[rows], jnp.where(quantity == 1, gc, 0.0))
        col_ref[rows, :] = out
        row_ref[:, rows] = out.T


def _gates(ba, alog_l, dtb_l, hb, chunk):
    t, width = ba.shape
    n_hg = width // V7X_LANES
    bt = min(1024, t)
    return pl.pallas_call(
        functools.partial(_gate_kernel, hb=hb, chunk=chunk),
        grid=(n_hg, t // bt),
        in_specs=[
            pl.BlockSpec((bt, V7X_LANES), lambda h, i: (i, h)),
            pl.BlockSpec((1, V7X_LANES), lambda h, i: (0, h)),
            pl.BlockSpec((1, V7X_LANES), lambda h, i: (0, h)),
        ],
        out_specs=[
            pl.BlockSpec((bt, V7X_LANES), lambda h, i: (i, h)),
            pl.BlockSpec((V7X_LANES, bt), lambda h, i: (h, i)),
        ],
        out_shape=[
            jax.ShapeDtypeStruct((t, width), F32),
            jax.ShapeDtypeStruct((width, t), F32),
        ],
        compiler_params=_params(("parallel", "parallel")),
        name="delta_gates",
    )(ba, alog_l, dtb_l)


def _unit_lower_inverses(lows, blk_mask, eye):
    c = lows[0].shape[0]
    nblk = c // SOLVE_BASE
    diags = [jnp.where(blk_mask, low, 0.0) for low in lows]
    rests = [(low - diag).astype(BF16) for low, diag in zip(lows, diags)]

    sub = lax.broadcasted_iota(jnp.int32, (SOLVE_BASE, c), 0)
    lane = lax.broadcasted_iota(jnp.int32, (SOLVE_BASE, c), 1)
    blk_start = (lane // SOLVE_BASE) * SOLVE_BASE
    compact_eye = (sub == lane - blk_start).astype(F32)

    def block_inverse(diag):
        lc = jnp.sum(diag.reshape(nblk, SOLVE_BASE, c), axis=0)
        x = compact_eye
        for j in range(SOLVE_BASE - 1):
            mult = jnp.take_along_axis(lc, blk_start + j, axis=1)
            x = x - mult * jnp.broadcast_to(x[j:j + 1, :], (SOLVE_BASE, c))
        return jnp.where(blk_mask, jnp.tile(x, (nblk, 1)), 0.0)

    invs = [block_inverse(diag) for diag in diags]
    zps = [_dot(inv.astype(BF16), rest).astype(BF16) for inv, rest in zip(invs, rests)]
    invs = [inv - _dot(zp, inv.astype(BF16)) for zp, inv in zip(zps, invs)]
    span = 2
    while span < nblk:
        zps = [_dot(zp, zp).astype(BF16) for zp in zps]
        invs = [inv + _dot(zp, inv.astype(BF16)) for zp, inv in zip(zps, invs)]
        span *= 2
    return invs


def _delta_kernel(q_ref, k_ref, v_ref, z_ref, cwq_ref, cwk_ref, cwv_ref, gcol_ref, grow_ref, ng_ref,
                  o_ref, qbuf, kbuf, vbuf, s_ref, *, hb, dk, kconv):
    c = q_ref.shape[0]
    pad = V7X_SUBLANES
    first = pl.program_id(2) == 0

    @pl.when(first)
    def _():
        for tail in (qbuf, kbuf, vbuf):
            tail[...] = jnp.zeros_like(tail)
        s_ref[...] = jnp.zeros_like(s_ref)

    def conv_silu(x_ref, tail_ref, cw_ref):
        cur = x_ref[...]
        width = cur.shape[1]
        tiles = jnp.concatenate([tail_ref[...], cur], axis=0).reshape(c // pad + 1, pad, width)
        sublane = lax.broadcasted_iota(jnp.int32, (1, pad, width), 1)
        cw = cw_ref[...]
        y = cur * cw[kconv - 1:kconv, :]
        for s in range(1, kconv):
            rot = pltpu.roll(tiles, s, 1)
            shifted = jnp.where(sublane < s, rot[:-1], rot[1:]).reshape(c, width)
            y = y + shifted * cw[kconv - 1 - s:kconv - s, :]
        tail_ref[...] = cur[c - pad:c, :]
        return _silu(y)

    qc = conv_silu(q_ref, qbuf, cwq_ref)
    kc = conv_silu(k_ref, kbuf, cwk_ref)
    vc = conv_silu(v_ref, vbuf, cwv_ref)

    ri = lax.broadcasted_iota(jnp.int32, (c, c), 0)
    ci = lax.broadcasted_iota(jnp.int32, (c, c), 1)
    causal = ri >= ci
    strict = ri > ci
    blk_mask = (ri // SOLVE_BASE) == (ci // SOLVE_BASE)
    eye = (ri == ci).astype(F32)
    gcol = gcol_ref[...]
    grow = grow_ref[...]
    ng = ng_ref[...]

    heads = range(hb)

    def head_cols(h):
        return slice(h * dk, (h + 1) * dk)

    def gate_col(quantity, h):
        lane = quantity * hb + h
        return gcol[:, lane:lane + 1]

    qn, kn, kb, rhs, decay, qd16, kd16, egl = [], [], [], [], [], [], [], []
    for h in heads:
        qh, kh, vh = qc[:, head_cols(h)], kc[:, head_cols(h)], vc[:, head_cols(h)]
        qn_h = qh * lax.rsqrt(jnp.sum(qh * qh, axis=-1, keepdims=True) + EPS) * (dk ** -0.5)
        kn_h = kh * lax.rsqrt(jnp.sum(kh * kh, axis=-1, keepdims=True) + EPS)
        beta = jnp.broadcast_to(gate_col(0, h), (c, dk))
        gc = jnp.broadcast_to(gate_col(1, h), (c, dk))
        gl = gc[c - 1:c, :]
        eg = jnp.exp(gc)
        ekd = jnp.exp(gl - gc)
        egl.append(jnp.exp(gl))
        gr = grow[hb + h:hb + h + 1, :]
        gc_cc = gc if c == dk else jnp.broadcast_to(gate_col(1, h), (c, c))
        decay.append(jnp.where(causal, jnp.exp(jnp.where(causal, gc_cc - gr, 0.0)), 0.0))
        kb_h = kn_h * beta
        qn.append(qn_h.astype(BF16))
        kn.append(kn_h.astype(BF16))
        kb.append(kb_h.astype(BF16))
        rhs.append(jnp.concatenate([(vh * beta).astype(BF16), (kb_h * eg).astype(BF16)], axis=1))
        qd16.append((qn_h * eg).astype(BF16))
        kd16.append((kn_h * ekd).astype(BF16))

    kk = [_dot_nt(kb[h], kn[h]) for h in heads]
    qk16 = [(_dot_nt(qn[h], kn[h]) * decay[h]).astype(BF16) for h in heads]
    lows = [jnp.where(strict, kk[h] * decay[h], 0.0) for h in heads]
    ainv = _unit_lower_inverses(lows, blk_mask, eye)
    uw = [_dot(ainv[h].astype(BF16), rhs[h]) for h in heads]
    s = [s_ref[h] for h in heads]
    s16 = [s[h].astype(BF16) for h in heads]
    ws_qs = [_dot(jnp.concatenate([uw[h][:, dk:].astype(BF16), qd16[h]], axis=0), s16[h]) for h in heads]
    vn16 = [(uw[h][:, :dk] - ws_qs[h][:c]).astype(BF16) for h in heads]
    o = [ws_qs[h][c:] + _dot(qk16[h], vn16[h]) for h in heads]
    for h in heads:
        s_ref[h] = s[h] * egl[h] + _dot_tn(kd16[h], vn16[h])
    for h in heads:
        on = o[h] * lax.rsqrt(jnp.mean(o[h] * o[h], axis=-1, keepdims=True) + EPS) * ng
        o_ref[:, head_cols(h)] = (on * _silu(z_ref[:, head_cols(h)])).astype(o_ref.dtype)


def _delta_rule(proj, conv_w, gcol, grow, dn_norm_g, bsz, seq, n_heads, dk, hb):
    t = proj.shape[0]
    c = DELTA_CHUNK
    nc = seq // c
    kconv = conv_w.shape[0]
    wblk = hb * dk
    n_hg = n_heads // hb
    width = n_heads * dk

    def tok(b, hg, n):
        return b * nc + n

    def proj_spec(part):
        return pl.BlockSpec((c, wblk), lambda b, hg, n: (tok(b, hg, n), part * n_hg + hg))

    def conv_spec(part):
        return pl.BlockSpec((kconv, wblk), lambda b, hg, n: (0, part * n_hg + hg))

    return pl.pallas_call(
        functools.partial(_delta_kernel, hb=hb, dk=dk, kconv=kconv),
        grid=(bsz, n_hg, nc),
        in_specs=[
            proj_spec(0), proj_spec(1), proj_spec(2), proj_spec(3),
            conv_spec(0), conv_spec(1), conv_spec(2),
            pl.BlockSpec((c, V7X_LANES), lambda b, hg, n: (tok(b, hg, n), hg)),
            pl.BlockSpec((V7X_LANES, c), lambda b, hg, n: (hg, tok(b, hg, n))),
            pl.BlockSpec((1, dk), lambda b, hg, n: (0, 0)),
        ],
        out_specs=pl.BlockSpec((c, wblk), lambda b, hg, n: (tok(b, hg, n), hg)),
        out_shape=jax.ShapeDtypeStruct((t, width), BF16),
        scratch_shapes=[
            pltpu.VMEM((V7X_SUBLANES, wblk), F32),
            pltpu.VMEM((V7X_SUBLANES, wblk), F32),
            pltpu.VMEM((V7X_SUBLANES, wblk), F32),
            pltpu.VMEM((hb, dk, dk), F32),
        ],
        compiler_params=_params(("parallel", "parallel", "arbitrary")),
        name="gated_delta_rule",
    )(proj, proj, proj, proj, conv_w, conv_w, conv_w, gcol, grow, dn_norm_g.reshape(1, dk))


def _pool_kernel(u_ref, w_ref, sc_ref, o_ref, tail_ref, *, gw, hist):
    bt = u_ref.shape[0]
    n = pl.program_id(1)

    @pl.when(n == 0)
    def _():
        tail_ref[...] = jnp.zeros_like(tail_ref)

    cur = u_ref[...]
    pos = n * bt + lax.broadcasted_iota(jnp.int32, (bt, 1), 0)
    for g, window in enumerate(POOL_WINDOWS):
        cols = slice(g * gw, (g + 1) * gw)
        cg = cur[:, cols]
        ext = jnp.concatenate([tail_ref[:, cols], cg], axis=0)
        span = 1
        while span < window:
            ext = ext + pltpu.roll(ext, span, 0)
            span *= 2
        count = jnp.minimum(pos + 1, window).astype(F32)
        pooled = ext[hist:, :] / count - cg
        p = _dot(pooled.astype(BF16), w_ref[g])
        p = p * lax.rsqrt(jnp.mean(p * p, axis=-1, keepdims=True) + EPS)
        o_ref[:, cols] = (p * sc_ref[:, cols]).astype(o_ref.dtype)
    tail_ref[...] = cur[bt - hist:bt, :]


def _multiscale_pool(pool_in, pool_w16, pool_scale, bsz, seq):
    t = pool_in.shape[0]
    ng, gw, _ = pool_w16.shape
    width = ng * gw
    hist = max(POOL_WINDOWS)
    assert all(w & (w - 1) == 0 for w in POOL_WINDOWS) and hist % V7X_SUBLANES == 0
    bt = min(256, seq)
    per_seq = seq // bt
    return pl.pallas_call(
        functools.partial(_pool_kernel, gw=gw, hist=hist),
        grid=(bsz, per_seq),
        in_specs=[
            pl.BlockSpec((bt, width), lambda b, n: (b * per_seq + n, 0)),
            pl.BlockSpec((ng, gw, gw), lambda b, n: (0, 0, 0)),
            pl.BlockSpec((1, width), lambda b, n: (0, 0)),
        ],
        out_specs=pl.BlockSpec((bt, width), lambda b, n: (b * per_seq + n, 0)),
        out_shape=jax.ShapeDtypeStruct((t, width), BF16),
        scratch_shapes=[pltpu.VMEM((hist, width), F32)],
        compiler_params=_params(("parallel", "arbitrary")),
        name="multiscale_pool",
    )(pool_in, pool_w16, pool_scale.reshape(1, width))


def _heads_per_step(n_heads):
    hb = min(n_heads, 16)
    while n_heads % hb:
        hb -= 1
    return hb


def _hybrid_layer(x2, bsz, seq, c, w_ada, b_ada, norm1_g, w_in, conv_w, a_log, dt_bias, dn_norm_g,
                  pool_w, pool_scale, w_out, norm2_g, w_ff1, w_ff2):
    d = x2.shape[1]
    n_heads = a_log.shape[0]
    dk = dn_norm_g.shape[0]
    dn_width = n_heads * dk
    pool_width = pool_scale.shape[0]
    hb = _heads_per_step(n_heads)
    n_hg = n_heads // hb
    assert N_GATE_QUANTITIES * hb <= V7X_LANES
    assert pool_width == dn_width and seq % DELTA_CHUNK == 0

    mod = _modulation(c, w_ada, b_ada)
    sh1, sc1, gt1, sh2, sc2, gt2 = [m.reshape(bsz, 1, d) for m in jnp.split(mod, N_MOD, axis=-1)]

    o2 = 4 * dn_width
    o3 = o2 + n_heads
    o4 = o3 + n_heads
    w_in_t = w_in.T
    w_pool_t = w_in_t[o4:]
    w_b = w_in[:, o2:o3].reshape(d, n_hg, hb)
    w_a = w_in[:, o3:o4].reshape(d, n_hg, hb)
    lane_pad = V7X_LANES - N_GATE_QUANTITIES * hb
    w_ba = jnp.concatenate([w_b] + [w_a] * (N_GATE_QUANTITIES - 1) + [jnp.zeros((d, n_hg, lane_pad), F32)],
                           axis=2).reshape(d, n_hg * V7X_LANES)

    def lane_pack(v):
        vv = v.astype(F32).reshape(n_hg, hb)
        return jnp.concatenate([jnp.zeros((n_hg, hb), F32)] + [vv] * (N_GATE_QUANTITIES - 1)
                               + [jnp.zeros((n_hg, lane_pad), F32)], axis=1).reshape(1, n_hg * V7X_LANES)

    h1 = _norm_mod(x2, norm1_g, sc1, sh1, seq)
    proj = _matmul_ws(h1, w_in_t, o2, F32, False, "in_proj", w_transposed=True)
    pool_in = _matmul_ws(h1, w_pool_t, pool_width, F32, False, "pool_proj", w_transposed=True)
    ba = _matmul_ws(h1, w_ba, n_hg * V7X_LANES, F32, False, "gate_proj")
    gcol, grow = _gates(ba, lane_pack(a_log), lane_pack(dt_bias), hb, DELTA_CHUNK)

    o = _delta_rule(proj, conv_w, gcol, grow, dn_norm_g, bsz, seq, n_heads, dk, hb)
    p = _multiscale_pool(pool_in, pool_w.astype(BF16), pool_scale, bsz, seq)

    x1 = _out_proj_residual(o, p, w_out, x2, gt1, seq)

    h2 = _norm_mod(x1, norm2_g, sc2, sh2, seq)
    a, w_ff2_16 = _matmul_ws(h2, w_ff1, w_ff1.shape[1], BF16, True, "mlp_up_relu2", side=w_ff2)
    return _mlp_down_residual(a, w_ff2_16, x1, gt2, seq)


def kernel(x, c, w_ada, b_ada, norm1_g, w_in, conv_w, a_log, dt_bias, dn_norm_g, pool_w, pool_scale,
           w_out, norm2_g, w_ff1, w_ff2, final_norm_g):
    bsz, seq, d = x.shape
    x2 = x.reshape(bsz * seq, d)
    for l in range(w_ada.shape[0]):
        x2 = _hybrid_layer(x2, bsz, seq, c, w_ada[l], b_ada[l], norm1_g[l], w_in[l], conv_w[l], a_log[l],
                           dt_bias[l], dn_norm_g[l], pool_w[l], pool_scale[l], w_out[l], norm2_g[l],
                           w_ff1[l], w_ff2[l])
    return _final_norm(x2, final_norm_g).reshape(bsz, seq, d)
```

```python
import functools

import jax
import jax.numpy as jnp
from jax import lax
from jax.experimental import pallas as pl
from jax.experimental.pallas import tpu as pltpu

F32 = jnp.float32
BF16 = jnp.bfloat16
EPS = 1e-6

V7X_LANES = 128
V7X_SUBLANES = 8
V7X_VMEM_LIMIT_BYTES = 60 * 1024 * 1024

POOL_WINDOWS = (2, 4, 8, 16)
DELTA_CHUNK = 128
SOLVE_BASE = 2 * V7X_SUBLANES
N_MOD = 6
N_GATE_QUANTITIES = 2


def _params(semantics):
    return pltpu.CompilerParams(dimension_semantics=semantics, vmem_limit_bytes=V7X_VMEM_LIMIT_BYTES)


def _silu(x):
    return x * jax.nn.sigmoid(x)


def _dot(a, b):
    return jnp.dot(a, b, preferred_element_type=F32)


def _dot_nt(a, b):
    return lax.dot_general(a, b, (((1,), (1,)), ((), ())), preferred_element_type=F32)


def _dot_tn(a, b):
    return lax.dot_general(a, b, (((0,), (0,)), ((), ())), preferred_element_type=F32)


def _mod_kernel(c_ref, w_ref, b_ref, o_ref):
    a = _silu(c_ref[...]).astype(BF16)
    o_ref[...] = _dot(a, w_ref[...].astype(BF16)) + b_ref[...]


def _modulation(c, w_ada, b_ada):
    bsz, d = c.shape
    n = w_ada.shape[1]
    rows = V7X_SUBLANES
    c_pad = jnp.zeros((rows, d), F32).at[:bsz].set(c)
    bn = min(512, n)
    out = pl.pallas_call(
        _mod_kernel,
        grid=(n // bn,),
        in_specs=[
            pl.BlockSpec((rows, d), lambda j: (0, 0)),
            pl.BlockSpec((d, bn), lambda j: (0, j)),
            pl.BlockSpec((1, bn), lambda j: (0, j)),
        ],
        out_specs=pl.BlockSpec((rows, bn), lambda j: (0, j)),
        out_shape=jax.ShapeDtypeStruct((rows, n), F32),
        compiler_params=_params(("parallel",)),
        name="adaln_modulation",
    )(c_pad, w_ada, b_ada.reshape(1, n))
    return out[:bsz]


def _norm_mod_kernel(x_ref, g_ref, sc_ref, sh_ref, o_ref):
    x = x_ref[...]
    y = x * lax.rsqrt(jnp.mean(x * x, axis=-1, keepdims=True) + EPS)
    y = y * g_ref[...]
    o_ref[...] = (y * (1.0 + sc_ref[0]) + sh_ref[0]).astype(o_ref.dtype)


def _norm_mod_gate_kernel(x_ref, g_ref, sc_ref, sh_ref, wg_ref, o_ref, og_ref):
    x = x_ref[...]
    y = x * lax.rsqrt(jnp.mean(x * x, axis=-1, keepdims=True) + EPS)
    y = y * g_ref[...]
    h = (y * (1.0 + sc_ref[0]) + sh_ref[0]).astype(o_ref.dtype)
    o_ref[...] = h
    og_ref[...] = _dot(h, wg_ref[...].astype(BF16))


def _norm_kernel(x_ref, g_ref, o_ref):
    x = x_ref[...]
    y = x * lax.rsqrt(jnp.mean(x * x, axis=-1, keepdims=True) + EPS)
    o_ref[...] = (y * g_ref[...]).astype(o_ref.dtype)


def _norm_mod(x2, g, sc, sh, seq, w_gate=None):
    t, d = x2.shape
    bt = min(512, seq)
    per_seq = seq // bt
    in_specs = [
        pl.BlockSpec((bt, d), lambda i: (i, 0)),
        pl.BlockSpec((1, d), lambda i: (0, 0)),
        pl.BlockSpec((1, 1, d), lambda i: (i // per_seq, 0, 0)),
        pl.BlockSpec((1, 1, d), lambda i: (i // per_seq, 0, 0)),
    ]
    out_specs = [pl.BlockSpec((bt, d), lambda i: (i, 0))]
    out_shape = [jax.ShapeDtypeStruct((t, d), BF16)]
    operands = [x2, g.reshape(1, d), sc, sh]
    if w_gate is not None:
        n_gate = w_gate.shape[1]
        in_specs.append(pl.BlockSpec((d, n_gate), lambda i: (0, 0)))
        out_specs.append(pl.BlockSpec((bt, n_gate), lambda i: (i, 0)))
        out_shape.append(jax.ShapeDtypeStruct((t, n_gate), F32))
        operands.append(w_gate)
    outs = pl.pallas_call(
        _norm_mod_kernel if w_gate is None else _norm_mod_gate_kernel,
        grid=(t // bt,),
        in_specs=in_specs,
        out_specs=out_specs,
        out_shape=out_shape,
        compiler_params=_params(("parallel",)),
        name="rmsnorm_modulate" if w_gate is None else "rmsnorm_modulate_gate_proj",
    )(*operands)
    return outs[0] if w_gate is None else tuple(outs)


def _final_norm(x2, g):
    t, d = x2.shape
    bt = min(512, t)
    return pl.pallas_call(
        _norm_kernel,
        grid=(t // bt,),
        in_specs=[pl.BlockSpec((bt, d), lambda i: (i, 0)), pl.BlockSpec((1, d), lambda i: (0, 0))],
        out_specs=pl.BlockSpec((bt, d), lambda i: (i, 0)),
        out_shape=jax.ShapeDtypeStruct((t, d), F32),
        compiler_params=_params(("parallel",)),
        name="final_rmsnorm",
    )(x2, g.reshape(1, d))


def _fit(block, dim):
    block = min(block, dim)
    while dim % block:
        block -= V7X_LANES
    return block


def _stage_weight_chunk(wchunk_ref, wslots, n_col_blocks):
    p, i = pl.program_id(0), pl.program_id(1)
    rows = wchunk_ref.shape[0]

    @pl.when(p < n_col_blocks)
    def _():
        start = pl.multiple_of(i * rows, rows)
        wslots[p % 2, pl.ds(start, rows), :] = wchunk_ref[...].astype(BF16)


def _mm_ws_kernel(a_ref, wchunk_ref, *rest, n_col_blocks, relu2, w_transposed, has_side):
    if has_side:
        side_ref, o_ref, side_out_ref, wslots = rest
    else:
        o_ref, wslots = rest
    _stage_weight_chunk(wchunk_ref, wslots, n_col_blocks)
    p = pl.program_id(0)

    @pl.when(p > 0)
    def _():
        w = wslots[(p - 1) % 2]
        acc = _dot_nt(a_ref[...], w) if w_transposed else _dot(a_ref[...], w)
        if relu2:
            acc = jnp.maximum(acc, 0.0)
            acc = acc * acc
        o_ref[...] = acc.astype(o_ref.dtype)
        if has_side:
            side_out_ref[...] = side_ref[...].astype(side_out_ref.dtype)


def _mm_ws_out_kernel(o_ref, p_ref, wchunk_ref, x_ref, gt_ref, out_ref, wslots, *, n_col_blocks):
    _stage_weight_chunk(wchunk_ref, wslots, n_col_blocks)
    p = pl.program_id(0)
    ko = o_ref.shape[1]

    @pl.when(p > 0)
    def _():
        slot = (p - 1) % 2
        acc = _dot(o_ref[...], wslots[slot, 0:ko, :]) + _dot(p_ref[...], wslots[slot, ko:, :])
        out_ref[...] = x_ref[...] + gt_ref[0] * acc


def _ws_geometry(m, k, n, bm, bn):
    bm, bn = _fit(bm, m), _fit(bn, n)
    n_row_blocks = m // bm
    rows = k // n_row_blocks
    assert rows * n_row_blocks == k and rows % (2 * V7X_SUBLANES) == 0
    return bm, bn, n_row_blocks, n // bn, rows


def _ws_maps(n_col_blocks):
    def row_block(p, i):
        return jnp.where(p == 0, 0, i)

    def chunk(p, i):
        return (jnp.where(p == n_col_blocks, 0, i), jnp.minimum(p, n_col_blocks - 1))

    def col_block(p, i):
        return jnp.maximum(p - 1, 0)

    return row_block, chunk, col_block


def _matmul_ws(a, w, n, out_dtype, relu2, name, w_transposed=False, side=None, bm=1024, bn=1024):
    m, k = a.shape
    bm, bn, n_row_blocks, n_col_blocks, rows = _ws_geometry(m, k, n, bm, bn)
    row_block, chunk, col_block = _ws_maps(n_col_blocks)
    if w_transposed:
        rows = bn // n_row_blocks
        assert rows * n_row_blocks == bn and rows % (2 * V7X_SUBLANES) == 0
        w_spec = pl.BlockSpec(
            (rows, k),
            lambda p, i: (jnp.minimum(p, n_col_blocks - 1) * n_row_blocks + jnp.where(p == n_col_blocks, 0, i), 0))
        slots = pltpu.VMEM((2, bn, k), BF16)
    else:
        w_spec = pl.BlockSpec((rows, bn), chunk)
        slots = pltpu.VMEM((2, k, bn), BF16)
    in_specs = [pl.BlockSpec((bm, k), lambda p, i: (row_block(p, i), 0)), w_spec]
    out_specs = [pl.BlockSpec((bm, bn), lambda p, i: (row_block(p, i), col_block(p, i)))]
    out_shape = [jax.ShapeDtypeStruct((m, n), out_dtype)]
    operands = [a, w]
    if side is not None:
        side_rows, side_cols = side.shape
        slab = side_rows // (n_col_blocks * n_row_blocks)
        assert slab * n_col_blocks * n_row_blocks == side_rows and slab % (2 * V7X_SUBLANES) == 0
        side_spec = pl.BlockSpec(
            (slab, side_cols), lambda p, i: (jnp.where(p == 0, 0, (p - 1) * n_row_blocks + i), 0))
        in_specs.append(side_spec)
        out_specs.append(side_spec)
        out_shape.append(jax.ShapeDtypeStruct(side.shape, BF16))
        operands.append(side)
    outs = pl.pallas_call(
        functools.partial(_mm_ws_kernel, n_col_blocks=n_col_blocks, relu2=relu2, w_transposed=w_transposed,
                          has_side=side is not None),
        grid=(n_col_blocks + 1, n_row_blocks),
        in_specs=in_specs,
        out_specs=out_specs,
        out_shape=out_shape,
        scratch_shapes=[slots],
        compiler_params=_params(("arbitrary", "arbitrary")),
        name=name,
    )(*operands)
    return outs[0] if side is None else tuple(outs)


def _out_proj_residual(o, p, w_out, x2, gt, seq, bm=1024, bn=1024):
    t, ko = o.shape
    k, d = w_out.shape
    assert ko + p.shape[1] == k
    bm, bn, n_row_blocks, n_col_blocks, rows = _ws_geometry(t, k, d, min(bm, seq), bn)
    per_seq = seq // bm
    row_block, chunk, col_block = _ws_maps(n_col_blocks)
    return pl.pallas_call(
        functools.partial(_mm_ws_out_kernel, n_col_blocks=n_col_blocks),
        grid=(n_col_blocks + 1, n_row_blocks),
        in_specs=[
            pl.BlockSpec((bm, ko), lambda p, i: (row_block(p, i), 0)),
            pl.BlockSpec((bm, k - ko), lambda p, i: (row_block(p, i), 0)),
            pl.BlockSpec((rows, bn), chunk),
            pl.BlockSpec((bm, bn), lambda p, i: (row_block(p, i), col_block(p, i))),
            pl.BlockSpec((1, 1, bn), lambda p, i: (row_block(p, i) // per_seq, 0, col_block(p, i))),
        ],
        out_specs=pl.BlockSpec((bm, bn), lambda p, i: (row_block(p, i), col_block(p, i))),
        out_shape=jax.ShapeDtypeStruct((t, d), F32),
        scratch_shapes=[pltpu.VMEM((2, k, bn), BF16)],
        compiler_params=_params(("arbitrary", "arbitrary")),
        name="out_proj_residual",
    )(o, p, w_out, x2, gt)


def _mm_down_kernel(a_ref, w_ref, x_ref, gt_ref, o_ref, *, nk):
    kk = pl.program_id(2)
    if nk == 1:
        o_ref[...] = x_ref[...] + gt_ref[0] * _dot(a_ref[...], w_ref[...])
        return

    @pl.when(kk == 0)
    def _():
        o_ref[...] = _dot(a_ref[...], w_ref[...])

    @pl.when(jnp.logical_and(kk > 0, kk < nk - 1))
    def _():
        o_ref[...] += _dot(a_ref[...], w_ref[...])

    @pl.when(kk == nk - 1)
    def _():
        o_ref[...] = x_ref[...] + gt_ref[0] * (o_ref[...] + _dot(a_ref[...], w_ref[...]))


def _mlp_down_residual(a, w, x2, gt, seq, bm=1024, bn=1024, bk=4096):
    t, k = a.shape
    d = w.shape[1]
    bm, bn, bk = min(bm, seq), min(bn, d), min(bk, k)
    per_seq = seq // bm
    nk = k // bk
    return pl.pallas_call(
        functools.partial(_mm_down_kernel, nk=nk),
        grid=(t // bm, d // bn, nk),
        in_specs=[
            pl.BlockSpec((bm, bk), lambda i, j, kk: (i, kk)),
            pl.BlockSpec((bk, bn), lambda i, j, kk: (kk, j)),
            pl.BlockSpec((bm, bn), lambda i, j, kk: (i, j)),
            pl.BlockSpec((1, 1, bn), lambda i, j, kk: (i // per_seq, 0, j)),
        ],
        out_specs=pl.BlockSpec((bm, bn), lambda i, j, kk: (i, j)),
        out_shape=jax.ShapeDtypeStruct((t, d), F32),
        compiler_params=_params(("parallel", "parallel", "arbitrary")),
        name="mlp_down_residual",
    )(a, w, x2, gt)


def _split3(x):
    hi = x.astype(BF16)
    r = x - hi.astype(F32)
    mid = r.astype(BF16)
    lo = (r - mid.astype(F32)).astype(BF16)
    return hi, mid, lo


def _gate_kernel(ba_ref, alog_ref, dtb_ref, col_ref, row_ref, *, hb, chunk):
    bt, lanes = ba_ref.shape
    x = ba_ref[...]
    beta = jax.nn.sigmoid(x)
    z = x + dtb_ref[...]
    softplus = jnp.maximum(z, 0.0) + jnp.log1p(jnp.exp(-jnp.abs(z)))
    g = -jnp.exp(alog_ref[...]) * softplus

    ri = lax.broadcasted_iota(jnp.int32, (chunk, chunk), 0)
    ci = lax.broadcasted_iota(jnp.int32, (chunk, chunk), 1)
    tril = (ri >= ci).astype(BF16)
    quantity = lax.broadcasted_iota(jnp.int32, (chunk, lanes), 1) // hb

    for c in range(bt // chunk):
        rows = slice(c * chunk, (c + 1) * chunk)
        hi, mid, lo = _split3(g[rows])
        gc = _dot(tril, hi) + _dot(tril, mid) + _dot(tril, lo)
        out = jnp.where(quantity == 0, beta[rows], jnp.where(quantity == 1, gc, 0.0))
        col_ref[rows, :] = out
        row_ref[:, rows] = out.T


def _gates(ba, alog_l, dtb_l, hb, chunk):
    t, width = ba.shape
    n_hg = width // V7X_LANES
    bt = min(1024, t)
    return pl.pallas_call(
        functools.partial(_gate_kernel, hb=hb, chunk=chunk),
        grid=(n_hg, t // bt),
        in_specs=[
            pl.BlockSpec((bt, V7X_LANES), lambda h, i: (i, h)),
            pl.BlockSpec((1, V7X_LANES), lambda h, i: (0, h)),
            pl.BlockSpec((1, V7X_LANES), lambda h, i: (0, h)),
        ],
        out_specs=[
            pl.BlockSpec((bt, V7X_LANES), lambda h, i: (i, h)),
            pl.BlockSpec((V7X_LANES, bt), lambda h, i: (h, i)),
        ],
        out_shape=[
            jax.ShapeDtypeStruct((t, width), F32),
            jax.ShapeDtypeStruct((width, t), F32),
        ],
        compiler_params=_params(("parallel", "parallel")),
        name="delta_gates",
    )(ba, alog_l, dtb_l)


def _unit_lower_inverses(lows, blk_mask, eye):
    c = lows[0].shape[0]
    nblk = c // SOLVE_BASE
    diags = [jnp.where(blk_mask, low, 0.0) for low in lows]
    rests = [(low - diag).astype(BF16) for low, diag in zip(lows, diags)]

    sub = lax.broadcasted_iota(jnp.int32, (SOLVE_BASE, c), 0)
    lane = lax.broadcasted_iota(jnp.int32, (SOLVE_BASE, c), 1)
    blk_start = (lane // SOLVE_BASE) * SOLVE_BASE
    compact_eye = (sub == lane - blk_start).astype(F32)

    def block_inverse(diag):
        lc = jnp.sum(diag.reshape(nblk, SOLVE_BASE, c), axis=0)
        x = compact_eye
        for j in range(SOLVE_BASE - 1):
            mult = jnp.take_along_axis(lc, blk_start + j, axis=1)
            x = x - mult * jnp.broadcast_to(x[j:j + 1, :], (SOLVE_BASE, c))
        return jnp.where(blk_mask, jnp.tile(x, (nblk, 1)), 0.0)

    invs = [block_inverse(diag) for diag in diags]
    zps = [_dot(inv.astype(BF16), rest).astype(BF16) for inv, rest in zip(invs, rests)]
    invs = [inv - _dot(zp, inv.astype(BF16)) for zp, inv in zip(zps, invs)]
    span = 2
    while span < nblk:
        zps = [_dot(zp, zp).astype(BF16) for zp in zps]
        invs = [inv + _dot(zp, inv.astype(BF16)) for zp, inv in zip(zps, invs)]
        span *= 2
    return invs


def _delta_kernel(q_ref, k_ref, v_ref, z_ref, cwq_ref, cwk_ref, cwv_ref, gcol_ref, grow_ref, ng_ref,
                  o_ref, qbuf, kbuf, vbuf, s_ref, sel_ref, *, hb, dk, kconv):
    c = q_ref.shape[0]
    assert c == dk == V7X_LANES
    pad = V7X_SUBLANES
    first = pl.program_id(2) == 0

    @pl.when(first)
    def _():
        for tail in (qbuf, kbuf, vbuf):
            tail[...] = jnp.zeros_like(tail)
        s_ref[...] = jnp.zeros_like(s_ref)
        lane_of_row = lax.broadcasted_iota(jnp.int32, (V7X_LANES, dk), 0)
        for lane in range(N_GATE_QUANTITIES * hb):
            sel_ref[lane] = (lane_of_row == lane).astype(BF16)

    def conv_silu(x_ref, tail_ref, cw_ref):
        cur = x_ref[...]
        width = cur.shape[1]
        tiles = jnp.concatenate([tail_ref[...], cur], axis=0).reshape(c // pad + 1, pad, width)
        sublane = lax.broadcasted_iota(jnp.int32, (1, pad, width), 1)
        cw = cw_ref[...]
        y = cur * cw[kconv - 1:kconv, :]
        for s in range(1, kconv):
            rot = pltpu.roll(tiles, s, 1)
            shifted = jnp.where(sublane < s, rot[:-1], rot[1:]).reshape(c, width)
            y = y + shifted * cw[kconv - 1 - s:kconv - s, :]
        tail_ref[...] = cur[c - pad:c, :]
        return _silu(y)

    qc = conv_silu(q_ref, qbuf, cwq_ref)
    kc = conv_silu(k_ref, kbuf, cwk_ref)
    vc = conv_silu(v_ref, vbuf, cwv_ref)

    ri = lax.broadcasted_iota(jnp.int32, (c, c), 0)
    ci = lax.broadcasted_iota(jnp.int32, (c, c), 1)
    causal = ri >= ci
    strict = ri > ci
    blk_mask = (ri // SOLVE_BASE) == (ci // SOLVE_BASE)
    eye = (ri == ci).astype(F32)
    grow = grow_ref[...]
    ng = ng_ref[...]
    heads = range(hb)

    def head_cols(h):
        return slice(h * dk, (h + 1) * dk)

    ones16 = jnp.ones((dk, dk), BF16)
    gate_pieces = _split3(gcol_ref[...])

    def gate_lanes(quantity, h):
        sel = sel_ref[quantity * hb + h]
        return _dot(gate_pieces[0], sel) + _dot(gate_pieces[1], sel) + _dot(gate_pieces[2], sel)

    def row_sum_lanes(v):
        hi = v.astype(BF16)
        lo = (v - hi.astype(F32)).astype(BF16)
        return _dot(hi, ones16) + _dot(lo, ones16)

    qn, kn, kb, rhs, decay, qd16, kd16, egl = [], [], [], [], [], [], [], []
    for h in heads:
        qh, kh, vh = qc[:, head_cols(h)], kc[:, head_cols(h)], vc[:, head_cols(h)]
        qn_h = qh * lax.rsqrt(row_sum_lanes(qh * qh) + EPS) * (dk ** -0.5)
        kn_h = kh * lax.rsqrt(row_sum_lanes(kh * kh) + EPS)
        beta = gate_lanes(0, h)
        gc = gate_lanes(1, h)
        gl = gc[c - 1:c, :]
        eg = jnp.exp(gc)
        ekd = jnp.exp(gl - gc)
        egl.append(jnp.exp(gl))
        gr = grow[hb + h:hb + h + 1, :]
        decay.append(jnp.where(causal, jnp.exp(jnp.where(causal, gc - gr, 0.0)), 0.0))
        kb_h = kn_h * beta
        qn.append(qn_h.astype(BF16))
        kn.append(kn_h.astype(BF16))
        kb.append(kb_h.astype(BF16))
        rhs.append(jnp.concatenate([(vh * beta).astype(BF16), (kb_h * eg).astype(BF16)], axis=1))
        qd16.append((qn_h * eg).astype(BF16))
        kd16.append((kn_h * ekd).astype(BF16))

    kk = [_dot_nt(kb[h], kn[h]) for h in heads]
    qk16 = [(_dot_nt(qn[h], kn[h]) * decay[h]).astype(BF16) for h in heads]
    lows = [jnp.where(strict, kk[h] * decay[h], 0.0) for h in heads]
    ainv = _unit_lower_inverses(lows, blk_mask, eye)
    uw = [_dot(ainv[h].astype(BF16), rhs[h]) for h in heads]
    s = [s_ref[h] for h in heads]
    s16 = [s[h].astype(BF16) for h in heads]
    ws_qs = [_dot(jnp.concatenate([uw[h][:, dk:].astype(BF16), qd16[h]], axis=0), s16[h]) for h in heads]
    vn16 = [(uw[h][:, :dk] - ws_qs[h][:c]).astype(BF16) for h in heads]
    o = [ws_qs[h][c:] + _dot(qk16[h], vn16[h]) for h in heads]
    for h in heads:
        s_ref[h] = s[h] * egl[h] + _dot_tn(kd16[h], vn16[h])
    for h in heads:
        on = o[h] * lax.rsqrt(row_sum_lanes(o[h] * o[h]) * (1.0 / dk) + EPS) * ng
        o_ref[:, head_cols(h)] = (on * _silu(z_ref[:, head_cols(h)])).astype(o_ref.dtype)


def _delta_rule(proj, conv_w, gcol, grow, dn_norm_g, bsz, seq, n_heads, dk, hb):
    t = proj.shape[0]
    c = DELTA_CHUNK
    nc = seq // c
    kconv = conv_w.shape[0]
    wblk = hb * dk
    n_hg = n_heads // hb
    width = n_heads * dk

    def tok(b, hg, n):
        return b * nc + n

    def proj_spec(part):
        return pl.BlockSpec((c, wblk), lambda b, hg, n: (tok(b, hg, n), part * n_hg + hg))

    def conv_spec(part):
        return pl.BlockSpec((kconv, wblk), lambda b, hg, n: (0, part * n_hg + hg))

    return pl.pallas_call(
        functools.partial(_delta_kernel, hb=hb, dk=dk, kconv=kconv),
        grid=(bsz, n_hg, nc),
        in_specs=[
            proj_spec(0), proj_spec(1), proj_spec(2), proj_spec(3),
            conv_spec(0), conv_spec(1), conv_spec(2),
            pl.BlockSpec((c, V7X_LANES), lambda b, hg, n: (tok(b, hg, n), hg)),
            pl.BlockSpec((V7X_LANES, c), lambda b, hg, n: (hg, tok(b, hg, n))),
            pl.BlockSpec((1, dk), lambda b, hg, n: (0, 0)),
        ],
        out_specs=pl.BlockSpec((c, wblk), lambda b, hg, n: (tok(b, hg, n), hg)),
        out_shape=jax.ShapeDtypeStruct((t, width), BF16),
        scratch_shapes=[
            pltpu.VMEM((V7X_SUBLANES, wblk), F32),
            pltpu.VMEM((V7X_SUBLANES, wblk), F32),
            pltpu.VMEM((V7X_SUBLANES, wblk), F32),
            pltpu.VMEM((hb, dk, dk), F32),
            pltpu.VMEM((N_GATE_QUANTITIES * hb, V7X_LANES, dk), BF16),
        ],
        compiler_params=_params(("parallel", "parallel", "arbitrary")),
        name="gated_delta_rule",
    )(proj, proj, proj, proj, conv_w, conv_w, conv_w, gcol, grow, dn_norm_g.reshape(1, dk))


def _pool_kernel(u_ref, w_ref, sc_ref, o_ref, tail_ref, *, gw, hist):
    bt = u_ref.shape[0]
    n = pl.program_id(1)

    @pl.when(n == 0)
    def _():
        tail_ref[...] = jnp.zeros_like(tail_ref)

    cur = u_ref[...]
    pos = n * bt + lax.broadcasted_iota(jnp.int32, (bt, 1), 0)
    for g, window in enumerate(POOL_WINDOWS):
        cols = slice(g * gw, (g + 1) * gw)
        cg = cur[:, cols]
        ext = jnp.concatenate([tail_ref[:, cols], cg], axis=0)
        span = 1
        while span < window:
            ext = ext + pltpu.roll(ext, span, 0)
            span *= 2
        count = jnp.minimum(pos + 1, window).astype(F32)
        pooled = ext[hist:, :] / count - cg
        p = _dot(pooled.astype(BF16), w_ref[g])
        p = p * lax.rsqrt(jnp.mean(p * p, axis=-1, keepdims=True) + EPS)
        o_ref[:, cols] = (p * sc_ref[:, cols]).astype(o_ref.dtype)
    tail_ref[...] = cur[bt - hist:bt, :]


def _multiscale_pool(pool_in, pool_w16, pool_scale, bsz, seq):
    t = pool_in.shape[0]
    ng, gw, _ = pool_w16.shape
    width = ng * gw
    hist = max(POOL_WINDOWS)
    assert all(w & (w - 1) == 0 for w in POOL_WINDOWS) and hist % V7X_SUBLANES == 0
    bt = min(256, seq)
    per_seq = seq // bt
    return pl.pallas_call(
        functools.partial(_pool_kernel, gw=gw, hist=hist),
        grid=(bsz, per_seq),
        in_specs=[
            pl.BlockSpec((bt, width), lambda b, n: (b * per_seq + n, 0)),
            pl.BlockSpec((ng, gw, gw), lambda b, n: (0, 0, 0)),
            pl.BlockSpec((1, width), lambda b, n: (0, 0)),
        ],
        out_specs=pl.BlockSpec((bt, width), lambda b, n: (b * per_seq + n, 0)),
        out_shape=jax.ShapeDtypeStruct((t, width), BF16),
        scratch_shapes=[pltpu.VMEM((hist, width), F32)],
        compiler_params=_params(("parallel", "arbitrary")),
        name="multiscale_pool",
    )(pool_in, pool_w16, pool_scale.reshape(1, width))


def _heads_per_step(n_heads):
    hb = min(n_heads, 16)
    while n_heads % hb:
        hb -= 1
    return hb


def _hybrid_layer(x2, bsz, seq, c, w_ada, b_ada, norm1_g, w_in, conv_w, a_log, dt_bias, dn_norm_g,
                  pool_w, pool_scale, w_out, norm2_g, w_ff1, w_ff2):
    d = x2.shape[1]
    n_heads = a_log.shape[0]
    dk = dn_norm_g.shape[0]
    dn_width = n_heads * dk
    pool_width = pool_scale.shape[0]
    hb = _heads_per_step(n_heads)
    n_hg = n_heads // hb
    assert N_GATE_QUANTITIES * hb <= V7X_LANES
    assert pool_width == dn_width and seq % DELTA_CHUNK == 0

    mod = _modulation(c, w_ada, b_ada)
    sh1, sc1, gt1, sh2, sc2, gt2 = [m.reshape(bsz, 1, d) for m in jnp.split(mod, N_MOD, axis=-1)]

    o2 = 4 * dn_width
    o3 = o2 + n_heads
    o4 = o3 + n_heads
    w_in_t = w_in.T
    w_pool_t = w_in_t[o4:]
    w_b = w_in[:, o2:o3].reshape(d, n_hg, hb)
    w_a = w_in[:, o3:o4].reshape(d, n_hg, hb)
    lane_pad = V7X_LANES - N_GATE_QUANTITIES * hb
    w_ba = jnp.concatenate([w_b] + [w_a] * (N_GATE_QUANTITIES - 1) + [jnp.zeros((d, n_hg, lane_pad), F32)],
                           axis=2).reshape(d, n_hg * V7X_LANES)

    def lane_pack(v):
        vv = v.astype(F32).reshape(n_hg, hb)
        return jnp.concatenate([jnp.zeros((n_hg, hb), F32)] + [vv] * (N_GATE_QUANTITIES - 1)
                               + [jnp.zeros((n_hg, lane_pad), F32)], axis=1).reshape(1, n_hg * V7X_LANES)

    h1, ba = _norm_mod(x2, norm1_g, sc1, sh1, seq, w_gate=w_ba)
    proj = _matmul_ws(h1, w_in_t, o2, F32, False, "in_proj", w_transposed=True)
    pool_in = _matmul_ws(h1, w_pool_t, pool_width, F32, False, "pool_proj", w_transposed=True)
    gcol, grow = _gates(ba, lane_pack(a_log), lane_pack(dt_bias), hb, DELTA_CHUNK)

    o = _delta_rule(proj, conv_w, gcol, grow, dn_norm_g, bsz, seq, n_heads, dk, hb)
    p = _multiscale_pool(pool_in, pool_w.astype(BF16), pool_scale, bsz, seq)

    x1 = _out_proj_residual(o, p, w_out, x2, gt1, seq)

    h2 = _norm_mod(x1, norm2_g, sc2, sh2, seq)
    a, w_ff2_16 = _matmul_ws(h2, w_ff1, w_ff1.shape[1], BF16, True, "mlp_up_relu2", side=w_ff2)
    return _mlp_down_residual(a, w_ff2_16, x1, gt2, seq)


def kernel(x, c, w_ada, b_ada, norm1_g, w_in, conv_w, a_log, dt_bias, dn_norm_g, pool_w, pool_scale,
           w_out, norm2_g, w_ff1, w_ff2, final_norm_g):
    bsz, seq, d = x.shape
    x2 = x.reshape(bsz * seq, d)
    for l in range(w_ada.shape[0]):
        x2 = _hybrid_layer(x2, bsz, seq, c, w_ada[l], b_ada[l], norm1_g[l], w_in[l], conv_w[l], a_log[l],
                           dt_bias[l], dn_norm_g[l], pool_w[l], pool_scale[l], w_out[l], norm2_g[l],
                           w_ff1[l], w_ff2[l])
    return _final_norm(x2, final_norm_g).reshape(bsz, seq, d)
```

```python
import functools

import jax
import jax.numpy as jnp
from jax import lax
from jax.experimental import pallas as pl
from jax.experimental.pallas import tpu as pltpu

F32 = jnp.float32
BF16 = jnp.bfloat16
EPS = 1e-6
LOG2_E = 1.4426950408889634

V7X_LANES = 128
V7X_SUBLANES = 8
V7X_VMEM_LIMIT_BYTES = 60 * 1024 * 1024

POOL_WINDOWS = (2, 4, 8, 16)
DELTA_CHUNK = 128
DELTA_CHUNKS_PER_STEP = 2
SOLVE_BASE = 2 * V7X_SUBLANES
N_MOD = 6
N_GATE_QUANTITIES = 2


def _params(semantics):
    return pltpu.CompilerParams(dimension_semantics=semantics, vmem_limit_bytes=V7X_VMEM_LIMIT_BYTES)


def _silu(x):
    return x * jax.nn.sigmoid(x)


def _dot(a, b):
    return jnp.dot(a, b, preferred_element_type=F32)


def _dot_nt(a, b):
    return lax.dot_general(a, b, (((1,), (1,)), ((), ())), preferred_element_type=F32)


def _dot_tn(a, b):
    return lax.dot_general(a, b, (((0,), (0,)), ((), ())), preferred_element_type=F32)


def _mod_kernel(c_ref, w_ref, b_ref, o_ref):
    a = _silu(c_ref[...]).astype(BF16)
    o_ref[...] = _dot(a, w_ref[...].astype(BF16)) + b_ref[...]


def _modulation(c, w_ada, b_ada):
    bsz, d = c.shape
    n = w_ada.shape[1]
    rows = V7X_SUBLANES
    c_pad = jnp.zeros((rows, d), F32).at[:bsz].set(c)
    bn = min(512, n)
    out = pl.pallas_call(
        _mod_kernel,
        grid=(n // bn,),
        in_specs=[
            pl.BlockSpec((rows, d), lambda j: (0, 0)),
            pl.BlockSpec((d, bn), lambda j: (0, j)),
            pl.BlockSpec((1, bn), lambda j: (0, j)),
        ],
        out_specs=pl.BlockSpec((rows, bn), lambda j: (0, j)),
        out_shape=jax.ShapeDtypeStruct((rows, n), F32),
        compiler_params=_params(("parallel",)),
        name="adaln_modulation",
    )(c_pad, w_ada, b_ada.reshape(1, n))
    return out[:bsz]


def _norm_mod_kernel(x_ref, g_ref, sc_ref, sh_ref, o_ref):
    x = x_ref[...]
    y = x * lax.rsqrt(jnp.mean(x * x, axis=-1, keepdims=True) + EPS)
    y = y * g_ref[...]
    o_ref[...] = (y * (1.0 + sc_ref[0]) + sh_ref[0]).astype(o_ref.dtype)


def _norm_mod_gate_kernel(x_ref, g_ref, sc_ref, sh_ref, wg_ref, o_ref, og_ref):
    x = x_ref[...]
    y = x * lax.rsqrt(jnp.mean(x * x, axis=-1, keepdims=True) + EPS)
    y = y * g_ref[...]
    h = (y * (1.0 + sc_ref[0]) + sh_ref[0]).astype(o_ref.dtype)
    o_ref[...] = h
    og_ref[...] = _dot(h, wg_ref[...].astype(BF16))


def _norm_kernel(x_ref, g_ref, o_ref):
    x = x_ref[...]
    y = x * lax.rsqrt(jnp.mean(x * x, axis=-1, keepdims=True) + EPS)
    o_ref[...] = (y * g_ref[...]).astype(o_ref.dtype)


def _norm_mod(x2, g, sc, sh, seq, w_gate=None):
    t, d = x2.shape
    bt = min(512, seq)
    per_seq = seq // bt
    in_specs = [
        pl.BlockSpec((bt, d), lambda i: (i, 0)),
        pl.BlockSpec((1, d), lambda i: (0, 0)),
        pl.BlockSpec((1, 1, d), lambda i: (i // per_seq, 0, 0)),
        pl.BlockSpec((1, 1, d), lambda i: (i // per_seq, 0, 0)),
    ]
    out_specs = [pl.BlockSpec((bt, d), lambda i: (i, 0))]
    out_shape = [jax.ShapeDtypeStruct((t, d), BF16)]
    operands = [x2, g.reshape(1, d), sc, sh]
    if w_gate is not None:
        n_gate = w_gate.shape[1]
        in_specs.append(pl.BlockSpec((d, n_gate), lambda i: (0, 0)))
        out_specs.append(pl.BlockSpec((bt, n_gate), lambda i: (i, 0)))
        out_shape.append(jax.ShapeDtypeStruct((t, n_gate), F32))
        operands.append(w_gate)
    outs = pl.pallas_call(
        _norm_mod_kernel if w_gate is None else _norm_mod_gate_kernel,
        grid=(t // bt,),
        in_specs=in_specs,
        out_specs=out_specs,
        out_shape=out_shape,
        compiler_params=_params(("parallel",)),
        name="rmsnorm_modulate" if w_gate is None else "rmsnorm_modulate_gate_proj",
    )(*operands)
    return outs[0] if w_gate is None else tuple(outs)


def _final_norm(x2, g):
    t, d = x2.shape
    bt = min(512, t)
    return pl.pallas_call(
        _norm_kernel,
        grid=(t // bt,),
        in_specs=[pl.BlockSpec((bt, d), lambda i: (i, 0)), pl.BlockSpec((1, d), lambda i: (0, 0))],
        out_specs=pl.BlockSpec((bt, d), lambda i: (i, 0)),
        out_shape=jax.ShapeDtypeStruct((t, d), F32),
        compiler_params=_params(("parallel",)),
        name="final_rmsnorm",
    )(x2, g.reshape(1, d))


def _fit(block, dim):
    block = min(block, dim)
    while dim % block:
        block -= V7X_LANES
    return block


def _stage_weight_chunk(wchunk_ref, wslots, n_col_blocks):
    p, i = pl.program_id(0), pl.program_id(1)
    rows = wchunk_ref.shape[0]

    @pl.when(p < n_col_blocks)
    def _():
        start = pl.multiple_of(i * rows, rows)
        wslots[p % 2, pl.ds(start, rows), :] = wchunk_ref[...].astype(BF16)


def _mm_ws_kernel(a_ref, wchunk_ref, *rest, n_col_blocks, relu2, w_transposed, has_side):
    if has_side:
        side_ref, o_ref, side_out_ref, wslots = rest
    else:
        o_ref, wslots = rest
    _stage_weight_chunk(wchunk_ref, wslots, n_col_blocks)
    p = pl.program_id(0)

    @pl.when(p > 0)
    def _():
        w = wslots[(p - 1) % 2]
        acc = _dot_nt(a_ref[...], w) if w_transposed else _dot(a_ref[...], w)
        if relu2:
            acc = jnp.maximum(acc, 0.0)
            acc = acc * acc
        o_ref[...] = acc.astype(o_ref.dtype)
        if has_side:
            side_out_ref[...] = side_ref[...].astype(side_out_ref.dtype)


def _mm_ws_out_kernel(o_ref, p_ref, wchunk_ref, x_ref, gt_ref, out_ref, wslots, *, n_col_blocks):
    _stage_weight_chunk(wchunk_ref, wslots, n_col_blocks)
    p = pl.program_id(0)
    ko = o_ref.shape[1]

    @pl.when(p > 0)
    def _():
        slot = (p - 1) % 2
        acc = _dot(o_ref[...], wslots[slot, 0:ko, :]) + _dot(p_ref[...], wslots[slot, ko:, :])
        out_ref[...] = x_ref[...] + gt_ref[0] * acc


def _ws_geometry(m, k, n, bm, bn):
    bm, bn = _fit(bm, m), _fit(bn, n)
    n_row_blocks = m // bm
    rows = k // n_row_blocks
    assert rows * n_row_blocks == k and rows % (2 * V7X_SUBLANES) == 0
    return bm, bn, n_row_blocks, n // bn, rows


def _ws_maps(n_col_blocks):
    def row_block(p, i):
        return jnp.where(p == 0, 0, i)

    def chunk(p, i):
        return (jnp.where(p == n_col_blocks, 0, i), jnp.minimum(p, n_col_blocks - 1))

    def col_block(p, i):
        return jnp.maximum(p - 1, 0)

    return row_block, chunk, col_block


def _matmul_ws(a, w, n, out_dtype, relu2, name, w_transposed=False, side=None, bm=1024, bn=1024):
    m, k = a.shape
    bm, bn, n_row_blocks, n_col_blocks, rows = _ws_geometry(m, k, n, bm, bn)
    row_block, chunk, col_block = _ws_maps(n_col_blocks)
    if w_transposed:
        rows = bn // n_row_blocks
        assert rows * n_row_blocks == bn and rows % (2 * V7X_SUBLANES) == 0
        w_spec = pl.BlockSpec(
            (rows, k),
            lambda p, i: (jnp.minimum(p, n_col_blocks - 1) * n_row_blocks + jnp.where(p == n_col_blocks, 0, i), 0))
        slots = pltpu.VMEM((2, bn, k), BF16)
    else:
        w_spec = pl.BlockSpec((rows, bn), chunk)
        slots = pltpu.VMEM((2, k, bn), BF16)
    in_specs = [pl.BlockSpec((bm, k), lambda p, i: (row_block(p, i), 0)), w_spec]
    out_specs = [pl.BlockSpec((bm, bn), lambda p, i: (row_block(p, i), col_block(p, i)))]
    out_shape = [jax.ShapeDtypeStruct((m, n), out_dtype)]
    operands = [a, w]
    if side is not None:
        side_rows, side_cols = side.shape
        slab = side_rows // (n_col_blocks * n_row_blocks)
        assert slab * n_col_blocks * n_row_blocks == side_rows and slab % (2 * V7X_SUBLANES) == 0
        side_spec = pl.BlockSpec(
            (slab, side_cols), lambda p, i: (jnp.where(p == 0, 0, (p - 1) * n_row_blocks + i), 0))
        in_specs.append(side_spec)
        out_specs.append(side_spec)
        out_shape.append(jax.ShapeDtypeStruct(side.shape, BF16))
        operands.append(side)
    outs = pl.pallas_call(
        functools.partial(_mm_ws_kernel, n_col_blocks=n_col_blocks, relu2=relu2, w_transposed=w_transposed,
                          has_side=side is not None),
        grid=(n_col_blocks + 1, n_row_blocks),
        in_specs=in_specs,
        out_specs=out_specs,
        out_shape=out_shape,
        scratch_shapes=[slots],
        compiler_params=_params(("arbitrary", "arbitrary")),
        name=name,
    )(*operands)
    return outs[0] if side is None else tuple(outs)


def _out_proj_residual(o, p, w_out, x2, gt, seq, bm=1024, bn=1024):
    t, ko = o.shape
    k, d = w_out.shape
    assert ko + p.shape[1] == k
    bm, bn, n_row_blocks, n_col_blocks, rows = _ws_geometry(t, k, d, min(bm, seq), bn)
    per_seq = seq // bm
    row_block, chunk, col_block = _ws_maps(n_col_blocks)
    return pl.pallas_call(
        functools.partial(_mm_ws_out_kernel, n_col_blocks=n_col_blocks),
        grid=(n_col_blocks + 1, n_row_blocks),
        in_specs=[
            pl.BlockSpec((bm, ko), lambda p, i: (row_block(p, i), 0)),
            pl.BlockSpec((bm, k - ko), lambda p, i: (row_block(p, i), 0)),
            pl.BlockSpec((rows, bn), chunk),
            pl.BlockSpec((bm, bn), lambda p, i: (row_block(p, i), col_block(p, i))),
            pl.BlockSpec((1, 1, bn), lambda p, i: (row_block(p, i) // per_seq, 0, col_block(p, i))),
        ],
        out_specs=pl.BlockSpec((bm, bn), lambda p, i: (row_block(p, i), col_block(p, i))),
        out_shape=jax.ShapeDtypeStruct((t, d), F32),
        scratch_shapes=[pltpu.VMEM((2, k, bn), BF16)],
        compiler_params=_params(("arbitrary", "arbitrary")),
        name="out_proj_residual",
    )(o, p, w_out, x2, gt)


def _mm_down_kernel(a_ref, w_ref, x_ref, gt_ref, o_ref, *, nk):
    kk = pl.program_id(2)
    if nk == 1:
        o_ref[...] = x_ref[...] + gt_ref[0] * _dot(a_ref[...], w_ref[...])
        return

    @pl.when(kk == 0)
    def _():
        o_ref[...] = _dot(a_ref[...], w_ref[...])

    @pl.when(jnp.logical_and(kk > 0, kk < nk - 1))
    def _():
        o_ref[...] += _dot(a_ref[...], w_ref[...])

    @pl.when(kk == nk - 1)
    def _():
        o_ref[...] = x_ref[...] + gt_ref[0] * (o_ref[...] + _dot(a_ref[...], w_ref[...]))


def _mlp_down_residual(a, w, x2, gt, seq, bm=1024, bn=1024, bk=4096):
    t, k = a.shape
    d = w.shape[1]
    bm, bn, bk = min(bm, seq), min(bn, d), min(bk, k)
    per_seq = seq // bm
    nk = k // bk
    return pl.pallas_call(
        functools.partial(_mm_down_kernel, nk=nk),
        grid=(t // bm, d // bn, nk),
        in_specs=[
            pl.BlockSpec((bm, bk), lambda i, j, kk: (i, kk)),
            pl.BlockSpec((bk, bn), lambda i, j, kk: (kk, j)),
            pl.BlockSpec((bm, bn), lambda i, j, kk: (i, j)),
            pl.BlockSpec((1, 1, bn), lambda i, j, kk: (i // per_seq, 0, j)),
        ],
        out_specs=pl.BlockSpec((bm, bn), lambda i, j, kk: (i, j)),
        out_shape=jax.ShapeDtypeStruct((t, d), F32),
        compiler_params=_params(("parallel", "parallel", "arbitrary")),
        name="mlp_down_residual",
    )(a, w, x2, gt)


def _split3(x):
    hi = x.astype(BF16)
    r = x - hi.astype(F32)
    mid = r.astype(BF16)
    lo = (r - mid.astype(F32)).astype(BF16)
    return hi, mid, lo


def _gate_kernel(ba_ref, alog_ref, dtb_ref, col_ref, row_ref, *, hb, chunk):
    bt, lanes = ba_ref.shape
    x = ba_ref[...]
    beta = jax.nn.sigmoid(x)
    z = x + dtb_ref[...]
    softplus = jnp.maximum(z, 0.0) + jnp.log1p(jnp.exp(-jnp.abs(z)))
    g = -jnp.exp(alog_ref[...]) * softplus * LOG2_E

    ri = lax.broadcasted_iota(jnp.int32, (chunk, chunk), 0)
    ci = lax.broadcasted_iota(jnp.int32, (chunk, chunk), 1)
    tril = (ri >= ci).astype(BF16)
    quantity = lax.broadcasted_iota(jnp.int32, (chunk, lanes), 1) // hb

    for c in range(bt // chunk):
        rows = slice(c * chunk, (c + 1) * chunk)
        hi, mid, lo = _split3(g[rows])
        gc = _dot(tril, hi) + _dot(tril, mid) + _dot(tril, lo)
        out = jnp.where(quantity == 0, beta[rows], jnp.where(quantity == 1, gc, 0.0))
        col_ref[rows, :] = out
        row_ref[:, rows] = out.T


def _gates(ba, alog_l, dtb_l, hb, chunk):
    t, width = ba.shape
    n_hg = width // V7X_LANES
    bt = min(1024, t)
    return pl.pallas_call(
        functools.partial(_gate_kernel, hb=hb, chunk=chunk),
        grid=(n_hg, t // bt),
        in_specs=[
            pl.BlockSpec((bt, V7X_LANES), lambda h, i: (i, h)),
            pl.BlockSpec((1, V7X_LANES), lambda h, i: (0, h)),
            pl.BlockSpec((1, V7X_LANES), lambda h, i: (0, h)),
        ],
        out_specs=[
            pl.BlockSpec((bt, V7X_LANES), lambda h, i: (i, h)),
            pl.BlockSpec((V7X_LANES, bt), lambda h, i: (h, i)),
        ],
        out_shape=[
            jax.ShapeDtypeStruct((t, width), F32),
            jax.ShapeDtypeStruct((width, t), F32),
        ],
        compiler_params=_params(("parallel", "parallel")),
        name="delta_gates",
    )(ba, alog_l, dtb_l)


def _unit_lower_inverses(lows, blk_mask):
    c = lows[0].shape[0]
    nblk = c // SOLVE_BASE
    diags = [jnp.where(blk_mask, low, 0.0) for low in lows]
    rests = [(low - diag).astype(BF16) for low, diag in zip(lows, diags)]

    sub = lax.broadcasted_iota(jnp.int32, (SOLVE_BASE, c), 0)
    lane = lax.broadcasted_iota(jnp.int32, (SOLVE_BASE, c), 1)
    blk_start = (lane // SOLVE_BASE) * SOLVE_BASE
    compact_eye = (sub == lane - blk_start).astype(F32)

    def block_inverse(diag):
        lc = jnp.sum(diag.reshape(nblk, SOLVE_BASE, c), axis=0)
        x = compact_eye
        for j in range(SOLVE_BASE - 1):
            mult = jnp.take_along_axis(lc, blk_start + j, axis=1)
            x = x - mult * jnp.broadcast_to(x[j:j + 1, :], (SOLVE_BASE, c))
        return jnp.where(blk_mask, jnp.tile(x, (nblk, 1)), 0.0)

    invs = [block_inverse(diag) for diag in diags]
    zps = [_dot(inv.astype(BF16), rest).astype(BF16) for inv, rest in zip(invs, rests)]
    invs = [inv - _dot(zp, inv.astype(BF16)) for zp, inv in zip(zps, invs)]
    span = 2
    while span < nblk:
        zps = [_dot(zp, zp).astype(BF16) for zp in zps]
        invs = [inv + _dot(zp, inv.astype(BF16)) for zp, inv in zip(zps, invs)]
        span *= 2
    return invs


def _delta_kernel(q_ref, k_ref, v_ref, z_ref, cwq_ref, cwk_ref, cwv_ref, gcol_ref, grow_ref, ng_ref,
                  o_ref, qbuf, kbuf, vbuf, s_ref, sel_ref, *, hb, dk, kconv):
    cb = q_ref.shape[0]
    c = DELTA_CHUNK
    subs = cb // c
    assert c == dk == V7X_LANES and subs * c == cb
    pad = V7X_SUBLANES
    first = pl.program_id(2) == 0

    @pl.when(first)
    def _():
        for tail in (qbuf, kbuf, vbuf):
            tail[...] = jnp.zeros_like(tail)
        s_ref[...] = jnp.zeros_like(s_ref)
        lane_of_row = lax.broadcasted_iota(jnp.int32, (V7X_LANES, dk), 0)
        for lane in range(N_GATE_QUANTITIES * hb):
            sel_ref[lane] = (lane_of_row == lane).astype(BF16)

    def conv_silu(x_ref, tail_ref, cw_ref):
        cur = x_ref[...]
        width = cur.shape[1]
        tiles = jnp.concatenate([tail_ref[...], cur], axis=0).reshape(cb // pad + 1, pad, width)
        sublane = lax.broadcasted_iota(jnp.int32, (1, pad, width), 1)
        cw = cw_ref[...]
        y = cur * cw[kconv - 1:kconv, :]
        for s in range(1, kconv):
            rot = pltpu.roll(tiles, s, 1)
            shifted = jnp.where(sublane < s, rot[:-1], rot[1:]).reshape(cb, width)
            y = y + shifted * cw[kconv - 1 - s:kconv - s, :]
        tail_ref[...] = cur[cb - pad:cb, :]
        return _silu(y)

    qc = conv_silu(q_ref, qbuf, cwq_ref)
    kc = conv_silu(k_ref, kbuf, cwk_ref)
    vc = conv_silu(v_ref, vbuf, cwv_ref)

    ri = lax.broadcasted_iota(jnp.int32, (c, c), 0)
    ci = lax.broadcasted_iota(jnp.int32, (c, c), 1)
    causal = ri >= ci
    strict = ri > ci
    blk_mask = (ri // SOLVE_BASE) == (ci // SOLVE_BASE)
    grow = grow_ref[...]
    ng = ng_ref[...]
    heads = range(hb)

    def head_cols(h):
        return slice(h * dk, (h + 1) * dk)

    ones16 = jnp.ones((dk, dk), BF16)
    gate_pieces = _split3(gcol_ref[...])

    def gate_lanes(quantity, h, rows):
        sel = sel_ref[quantity * hb + h]
        return sum(_dot(piece[rows], sel) for piece in gate_pieces)

    def row_sum_lanes(v):
        hi = v.astype(BF16)
        lo = (v - hi.astype(F32)).astype(BF16)
        return _dot(hi, ones16) + _dot(lo, ones16)

    units = [(sub, h) for sub in range(subs) for h in heads]
    qn, kn, kb, rhs, decay, qd16, kd16, egl = [], [], [], [], [], [], [], []
    for sub, h in units:
        rows = slice(sub * c, (sub + 1) * c)
        qh, kh, vh = qc[rows, head_cols(h)], kc[rows, head_cols(h)], vc[rows, head_cols(h)]
        qn_h = qh * lax.rsqrt(row_sum_lanes(qh * qh) + EPS) * (dk ** -0.5)
        kn_h = kh * lax.rsqrt(row_sum_lanes(kh * kh) + EPS)
        beta = gate_lanes(0, h, rows)
        gc = gate_lanes(1, h, rows)
        gl = gc[c - 1:c, :]
        eg = jnp.exp2(gc)
        ekd = jnp.exp2(gl - gc)
        egl.append(jnp.exp2(gl))
        gr = grow[hb + h:hb + h + 1, rows]
        decay.append(jnp.where(causal, jnp.exp2(jnp.where(causal, gc - gr, 0.0)), 0.0))
        kb_h = kn_h * beta
        qn.append(qn_h.astype(BF16))
        kn.append(kn_h.astype(BF16))
        kb.append(kb_h.astype(BF16))
        rhs.append(jnp.concatenate([(vh * beta).astype(BF16), (kb_h * eg).astype(BF16)], axis=1))
        qd16.append((qn_h * eg).astype(BF16))
        kd16.append((kn_h * ekd).astype(BF16))

    kk = [_dot_nt(kb_u, kn_u) for kb_u, kn_u in zip(kb, kn)]
    qk16 = [(_dot_nt(qn_u, kn_u) * dec_u).astype(BF16) for qn_u, kn_u, dec_u in zip(qn, kn, decay)]
    lows = [jnp.where(strict, kk_u * dec_u, 0.0) for kk_u, dec_u in zip(kk, decay)]
    ainv = _unit_lower_inverses(lows, blk_mask)
    uw = [_dot(ainv_u.astype(BF16), rhs_u) for ainv_u, rhs_u in zip(ainv, rhs)]

    s = [s_ref[h] for h in heads]
    for sub in range(subs):
        rows = slice(sub * c, (sub + 1) * c)
        unit = [sub * hb + h for h in heads]
        s16 = [s[h].astype(BF16) for h in heads]
        ws_qs = [_dot(jnp.concatenate([uw[unit[h]][:, dk:].astype(BF16), qd16[unit[h]]], axis=0), s16[h])
                 for h in heads]
        vn16 = [(uw[unit[h]][:, :dk] - ws_qs[h][:c]).astype(BF16) for h in heads]
        o = [ws_qs[h][c:] + _dot(qk16[unit[h]], vn16[h]) for h in heads]
        s = [s[h] * egl[unit[h]] + _dot_tn(kd16[unit[h]], vn16[h]) for h in heads]
        for h in heads:
            on = o[h] * lax.rsqrt(row_sum_lanes(o[h] * o[h]) * (1.0 / dk) + EPS) * ng
            o_ref[rows, head_cols(h)] = (on * _silu(z_ref[rows, head_cols(h)])).astype(o_ref.dtype)
    for h in heads:
        s_ref[h] = s[h]


def _delta_rule(proj, conv_w, gcol, grow, dn_norm_g, bsz, seq, n_heads, dk, hb):
    t = proj.shape[0]
    c = DELTA_CHUNK * DELTA_CHUNKS_PER_STEP
    assert seq % c == 0
    nc = seq // c
    kconv = conv_w.shape[0]
    wblk = hb * dk
    n_hg = n_heads // hb
    width = n_heads * dk

    def tok(b, hg, n):
        return b * nc + n

    def proj_spec(part):
        return pl.BlockSpec((c, wblk), lambda b, hg, n: (tok(b, hg, n), part * n_hg + hg))

    def conv_spec(part):
        return pl.BlockSpec((kconv, wblk), lambda b, hg, n: (0, part * n_hg + hg))

    return pl.pallas_call(
        functools.partial(_delta_kernel, hb=hb, dk=dk, kconv=kconv),
        grid=(bsz, n_hg, nc),
        in_specs=[
            proj_spec(0), proj_spec(1), proj_spec(2), proj_spec(3),
            conv_spec(0), conv_spec(1), conv_spec(2),
            pl.BlockSpec((c, V7X_LANES), lambda b, hg, n: (tok(b, hg, n), hg)),
            pl.BlockSpec((V7X_LANES, c), lambda b, hg, n: (hg, tok(b, hg, n))),
            pl.BlockSpec((1, dk), lambda b, hg, n: (0, 0)),
        ],
        out_specs=pl.BlockSpec((c, wblk), lambda b, hg, n: (tok(b, hg, n), hg)),
        out_shape=jax.ShapeDtypeStruct((t, width), BF16),
        scratch_shapes=[
            pltpu.VMEM((V7X_SUBLANES, wblk), F32),
            pltpu.VMEM((V7X_SUBLANES, wblk), F32),
            pltpu.VMEM((V7X_SUBLANES, wblk), F32),
            pltpu.VMEM((hb, dk, dk), F32),
            pltpu.VMEM((N_GATE_QUANTITIES * hb, V7X_LANES, dk), BF16),
        ],
        compiler_params=_params(("parallel", "parallel", "arbitrary")),
        name="gated_delta_rule",
    )(proj, proj, proj, proj, conv_w, conv_w, conv_w, gcol, grow, dn_norm_g.reshape(1, dk))


def _pool_kernel(u_ref, w_ref, sc_ref, o_ref, tail_ref, *, gw, hist):
    bt = u_ref.shape[0]
    n = pl.program_id(1)

    @pl.when(n == 0)
    def _():
        tail_ref[...] = jnp.zeros_like(tail_ref)

    cur = u_ref[...]
    pos = n * bt + lax.broadcasted_iota(jnp.int32, (bt, 1), 0)
    for g, window in enumerate(POOL_WINDOWS):
        cols = slice(g * gw, (g + 1) * gw)
        cg = cur[:, cols]
        ext = jnp.concatenate([tail_ref[:, cols], cg], axis=0)
        span = 1
        while span < window:
            ext = ext + pltpu.roll(ext, span, 0)
            span *= 2
        count = jnp.minimum(pos + 1, window).astype(F32)
        pooled = ext[hist:, :] / count - cg
        p = _dot(pooled.astype(BF16), w_ref[g])
        p = p * lax.rsqrt(jnp.mean(p * p, axis=-1, keepdims=True) + EPS)
        o_ref[:, cols] = (p * sc_ref[:, cols]).astype(o_ref.dtype)
    tail_ref[...] = cur[bt - hist:bt, :]


def _multiscale_pool(pool_in, pool_w16, pool_scale, bsz, seq):
    t = pool_in.shape[0]
    ng, gw, _ = pool_w16.shape
    width = ng * gw
    hist = max(POOL_WINDOWS)
    assert all(w & (w - 1) == 0 for w in POOL_WINDOWS) and hist % V7X_SUBLANES == 0
    bt = min(256, seq)
    per_seq = seq // bt
    return pl.pallas_call(
        functools.partial(_pool_kernel, gw=gw, hist=hist),
        grid=(bsz, per_seq),
        in_specs=[
            pl.BlockSpec((bt, width), lambda b, n: (b * per_seq + n, 0)),
            pl.BlockSpec((ng, gw, gw), lambda b, n: (0, 0, 0)),
            pl.BlockSpec((1, width), lambda b, n: (0, 0)),
        ],
        out_specs=pl.BlockSpec((bt, width), lambda b, n: (b * per_seq + n, 0)),
        out_shape=jax.ShapeDtypeStruct((t, width), BF16),
        scratch_shapes=[pltpu.VMEM((hist, width), F32)],
        compiler_params=_params(("parallel", "arbitrary")),
        name="multiscale_pool",
    )(pool_in, pool_w16, pool_scale.reshape(1, width))


def _heads_per_step(n_heads):
    hb = min(n_heads, 16)
    while n_heads % hb:
        hb -= 1
    return hb


def _hybrid_layer(x2, bsz, seq, c, w_ada, b_ada, norm1_g, w_in, conv_w, a_log, dt_bias, dn_norm_g,
                  pool_w, pool_scale, w_out, norm2_g, w_ff1, w_ff2):
    d = x2.shape[1]
    n_heads = a_log.shape[0]
    dk = dn_norm_g.shape[0]
    dn_width = n_heads * dk
    pool_width = pool_scale.shape[0]
    hb = _heads_per_step(n_heads)
    n_hg = n_heads // hb
    assert N_GATE_QUANTITIES * hb <= V7X_LANES
    assert pool_width == dn_width

    mod = _modulation(c, w_ada, b_ada)
    sh1, sc1, gt1, sh2, sc2, gt2 = [m.reshape(bsz, 1, d) for m in jnp.split(mod, N_MOD, axis=-1)]

    o2 = 4 * dn_width
    o3 = o2 + n_heads
    o4 = o3 + n_heads
    w_in_t = w_in.T
    w_pool_t = w_in_t[o4:]
    w_b = w_in[:, o2:o3].reshape(d, n_hg, hb)
    w_a = w_in[:, o3:o4].reshape(d, n_hg, hb)
    lane_pad = V7X_LANES - N_GATE_QUANTITIES * hb
    w_ba = jnp.concatenate([w_b] + [w_a] * (N_GATE_QUANTITIES - 1) + [jnp.zeros((d, n_hg, lane_pad), F32)],
                           axis=2).reshape(d, n_hg * V7X_LANES)

    def lane_pack(v):
        vv = v.astype(F32).reshape(n_hg, hb)
        return jnp.concatenate([jnp.zeros((n_hg, hb), F32)] + [vv] * (N_GATE_QUANTITIES - 1)
                               + [jnp.zeros((n_hg, lane_pad), F32)], axis=1).reshape(1, n_hg * V7X_LANES)

    h1, ba = _norm_mod(x2, norm1_g, sc1, sh1, seq, w_gate=w_ba)
    proj = _matmul_ws(h1, w_in_t, o2, F32, False, "in_proj", w_transposed=True)
    pool_in = _matmul_ws(h1, w_pool_t, pool_width, F32, False, "pool_proj", w_transposed=True)
    gcol, grow = _gates(ba, lane_pack(a_log), lane_pack(dt_bias), hb, DELTA_CHUNK)

    o = _delta_rule(proj, conv_w, gcol, grow, dn_norm_g, bsz, seq, n_heads, dk, hb)
    p = _multiscale_pool(pool_in, pool_w.astype(BF16), pool_scale, bsz, seq)

    x1 = _out_proj_residual(o, p, w_out, x2, gt1, seq)

    h2 = _norm_mod(x1, norm2_g, sc2, sh2, seq)
    a, w_ff2_16 = _matmul_ws(h2, w_ff1, w_ff1.shape[1], BF16, True, "mlp_up_relu2", side=w_ff2)
    return _mlp_down_residual(a, w_ff2_16, x1, gt2, seq)


def kernel(x, c, w_ada, b_ada, norm1_g, w_in, conv_w, a_log, dt_bias, dn_norm_g, pool_w, pool_scale,
           w_out, norm2_g, w_ff1, w_ff2, final_norm_g):
    bsz, seq, d = x.shape
    x2 = x.reshape(bsz * seq, d)
    for l in range(w_ada.shape[0]):
        x2 = _hybrid_layer(x2, bsz, seq, c, w_ada[l], b_ada[l], norm1_g[l], w_in[l], conv_w[l], a_log[l],
                           dt_bias[l], dn_norm_g[l], pool_w[l], pool_scale[l], w_out[l], norm2_g[l],
                           w_ff1[l], w_ff2[l])
    return _final_norm(x2, final_norm_g).reshape(bsz, seq, d)
```

```python
import functools

import jax
import jax.numpy as jnp
from jax import lax
from jax.experimental import pallas as pl
from jax.experimental.pallas import tpu as pltpu

F32 = jnp.float32
BF16 = jnp.bfloat16
EPS = 1e-6
LOG2_E = 1.4426950408889634

V7X_LANES = 128
V7X_SUBLANES = 8
V7X_VMEM_LIMIT_BYTES = 60 * 1024 * 1024

POOL_WINDOWS = (2, 4, 8, 16)
DELTA_CHUNK = 128
DELTA_CHUNKS_PER_STEP = 2
SOLVE_BASE = 2 * V7X_SUBLANES
N_MOD = 6
N_GATE_QUANTITIES = 2


def _params(semantics):
    return pltpu.CompilerParams(dimension_semantics=semantics, vmem_limit_bytes=V7X_VMEM_LIMIT_BYTES)


def _silu(x):
    return x * jax.nn.sigmoid(x)


def _dot(a, b):
    return jnp.dot(a, b, preferred_element_type=F32)


def _dot_nt(a, b):
    return lax.dot_general(a, b, (((1,), (1,)), ((), ())), preferred_element_type=F32)


def _dot_tn(a, b):
    return lax.dot_general(a, b, (((0,), (0,)), ((), ())), preferred_element_type=F32)


def _mod_kernel(c_ref, w_ref, b_ref, o_ref):
    a = _silu(c_ref[...]).astype(BF16)
    o_ref[...] = _dot(a, w_ref[...].astype(BF16)) + b_ref[...]


def _modulation(c, w_ada, b_ada):
    bsz, d = c.shape
    n = w_ada.shape[1]
    rows = V7X_SUBLANES
    c_pad = jnp.zeros((rows, d), F32).at[:bsz].set(c)
    bn = min(1024, n)
    out = pl.pallas_call(
        _mod_kernel,
        grid=(n // bn,),
        in_specs=[
            pl.BlockSpec((rows, d), lambda j: (0, 0)),
            pl.BlockSpec((d, bn), lambda j: (0, j)),
            pl.BlockSpec((1, bn), lambda j: (0, j)),
        ],
        out_specs=pl.BlockSpec((rows, bn), lambda j: (0, j)),
        out_shape=jax.ShapeDtypeStruct((rows, n), F32),
        compiler_params=_params(("parallel",)),
        name="adaln_modulation",
    )(c_pad, w_ada, b_ada.reshape(1, n))
    return out[:bsz]


def _norm_mod_kernel(x_ref, g_ref, sc_ref, sh_ref, o_ref):
    x = x_ref[...]
    y = x * lax.rsqrt(jnp.mean(x * x, axis=-1, keepdims=True) + EPS)
    y = y * g_ref[...]
    o_ref[...] = (y * (1.0 + sc_ref[0]) + sh_ref[0]).astype(o_ref.dtype)


def _norm_mod_gate_kernel(x_ref, g_ref, sc_ref, sh_ref, wg_ref, o_ref, og_ref):
    x = x_ref[...]
    y = x * lax.rsqrt(jnp.mean(x * x, axis=-1, keepdims=True) + EPS)
    y = y * g_ref[...]
    h = (y * (1.0 + sc_ref[0]) + sh_ref[0]).astype(o_ref.dtype)
    o_ref[...] = h
    og_ref[...] = _dot(h, wg_ref[...].astype(BF16))


def _norm_kernel(x_ref, g_ref, o_ref):
    x = x_ref[...]
    y = x * lax.rsqrt(jnp.mean(x * x, axis=-1, keepdims=True) + EPS)
    o_ref[...] = (y * g_ref[...]).astype(o_ref.dtype)


def _norm_mod(x2, g, sc, sh, seq, w_gate=None):
    t, d = x2.shape
    bt = min(512, seq)
    per_seq = seq // bt
    in_specs = [
        pl.BlockSpec((bt, d), lambda i: (i, 0)),
        pl.BlockSpec((1, d), lambda i: (0, 0)),
        pl.BlockSpec((1, 1, d), lambda i: (i // per_seq, 0, 0)),
        pl.BlockSpec((1, 1, d), lambda i: (i // per_seq, 0, 0)),
    ]
    out_specs = [pl.BlockSpec((bt, d), lambda i: (i, 0))]
    out_shape = [jax.ShapeDtypeStruct((t, d), BF16)]
    operands = [x2, g.reshape(1, d), sc, sh]
    if w_gate is not None:
        n_gate = w_gate.shape[1]
        in_specs.append(pl.BlockSpec((d, n_gate), lambda i: (0, 0)))
        out_specs.append(pl.BlockSpec((bt, n_gate), lambda i: (i, 0)))
        out_shape.append(jax.ShapeDtypeStruct((t, n_gate), F32))
        operands.append(w_gate)
    outs = pl.pallas_call(
        _norm_mod_kernel if w_gate is None else _norm_mod_gate_kernel,
        grid=(t // bt,),
        in_specs=in_specs,
        out_specs=out_specs,
        out_shape=out_shape,
        compiler_params=_params(("parallel",)),
        name="rmsnorm_modulate" if w_gate is None else "rmsnorm_modulate_gate_proj",
    )(*operands)
    return outs[0] if w_gate is None else tuple(outs)


def _final_norm(x2, g):
    t, d = x2.shape
    bt = min(512, t)
    return pl.pallas_call(
        _norm_kernel,
        grid=(t // bt,),
        in_specs=[pl.BlockSpec((bt, d), lambda i: (i, 0)), pl.BlockSpec((1, d), lambda i: (0, 0))],
        out_specs=pl.BlockSpec((bt, d), lambda i: (i, 0)),
        out_shape=jax.ShapeDtypeStruct((t, d), F32),
        compiler_params=_params(("parallel",)),
        name="final_rmsnorm",
    )(x2, g.reshape(1, d))


def _fit(block, dim):
    block = min(block, dim)
    while dim % block:
        block -= V7X_LANES
    return block


def _stage_weight_chunk(wchunk_ref, wslots, n_col_blocks):
    p, i = pl.program_id(0), pl.program_id(1)
    rows = wchunk_ref.shape[0]

    @pl.when(p < n_col_blocks)
    def _():
        start = pl.multiple_of(i * rows, rows)
        wslots[p % 2, pl.ds(start, rows), :] = wchunk_ref[...].astype(BF16)


def _mm_ws_kernel(a_ref, wchunk_ref, *rest, n_col_blocks, relu2, w_transposed, has_side):
    if has_side:
        side_ref, o_ref, side_out_ref, wslots = rest
    else:
        o_ref, wslots = rest
    _stage_weight_chunk(wchunk_ref, wslots, n_col_blocks)
    p = pl.program_id(0)

    @pl.when(p > 0)
    def _():
        w = wslots[(p - 1) % 2]
        acc = _dot_nt(a_ref[...], w) if w_transposed else _dot(a_ref[...], w)
        if relu2:
            acc = jnp.maximum(acc, 0.0)
            acc = acc * acc
        o_ref[...] = acc.astype(o_ref.dtype)
        if has_side:
            side_out_ref[...] = side_ref[...].astype(side_out_ref.dtype)


def _mm_ws_out_kernel(o_ref, p_ref, wchunk_ref, x_ref, gt_ref, out_ref, wslots, *, n_col_blocks):
    _stage_weight_chunk(wchunk_ref, wslots, n_col_blocks)
    p = pl.program_id(0)
    ko = o_ref.shape[1]

    @pl.when(p > 0)
    def _():
        slot = (p - 1) % 2
        acc = _dot(o_ref[...], wslots[slot, 0:ko, :]) + _dot(p_ref[...], wslots[slot, ko:, :])
        out_ref[...] = x_ref[...] + gt_ref[0] * acc


def _ws_geometry(m, k, n, bm, bn):
    bm, bn = _fit(bm, m), _fit(bn, n)
    n_row_blocks = m // bm
    rows = k // n_row_blocks
    assert rows * n_row_blocks == k and rows % (2 * V7X_SUBLANES) == 0
    return bm, bn, n_row_blocks, n // bn, rows


def _ws_maps(n_col_blocks):
    def row_block(p, i):
        return jnp.where(p == 0, 0, i)

    def chunk(p, i):
        return (jnp.where(p == n_col_blocks, 0, i), jnp.minimum(p, n_col_blocks - 1))

    def col_block(p, i):
        return jnp.maximum(p - 1, 0)

    return row_block, chunk, col_block


def _matmul_ws(a, w, n, out_dtype, relu2, name, w_transposed=False, side=None, bm=1024, bn=1024):
    m, k = a.shape
    bm, bn, n_row_blocks, n_col_blocks, rows = _ws_geometry(m, k, n, bm, bn)
    row_block, chunk, col_block = _ws_maps(n_col_blocks)
    if w_transposed:
        rows = bn // n_row_blocks
        assert rows * n_row_blocks == bn and rows % (2 * V7X_SUBLANES) == 0
        w_spec = pl.BlockSpec(
            (rows, k),
            lambda p, i: (jnp.minimum(p, n_col_blocks - 1) * n_row_blocks + jnp.where(p == n_col_blocks, 0, i), 0))
        slots = pltpu.VMEM((2, bn, k), BF16)
    else:
        w_spec = pl.BlockSpec((rows, bn), chunk)
        slots = pltpu.VMEM((2, k, bn), BF16)
    in_specs = [pl.BlockSpec((bm, k), lambda p, i: (row_block(p, i), 0)), w_spec]
    out_specs = [pl.BlockSpec((bm, bn), lambda p, i: (row_block(p, i), col_block(p, i)))]
    out_shape = [jax.ShapeDtypeStruct((m, n), out_dtype)]
    operands = [a, w]
    if side is not None:
        side_rows, side_cols = side.shape
        slab = side_rows // (n_col_blocks * n_row_blocks)
        assert slab * n_col_blocks * n_row_blocks == side_rows and slab % (2 * V7X_SUBLANES) == 0
        side_spec = pl.BlockSpec(
            (slab, side_cols), lambda p, i: (jnp.where(p == 0, 0, (p - 1) * n_row_blocks + i), 0))
        in_specs.append(side_spec)
        out_specs.append(side_spec)
        out_shape.append(jax.ShapeDtypeStruct(side.shape, BF16))
        operands.append(side)
    outs = pl.pallas_call(
        functools.partial(_mm_ws_kernel, n_col_blocks=n_col_blocks, relu2=relu2, w_transposed=w_transposed,
                          has_side=side is not None),
        grid=(n_col_blocks + 1, n_row_blocks),
        in_specs=in_specs,
        out_specs=out_specs,
        out_shape=out_shape,
        scratch_shapes=[slots],
        compiler_params=_params(("arbitrary", "arbitrary")),
        name=name,
    )(*operands)
    return outs[0] if side is None else tuple(outs)


def _out_proj_residual(o, p, w_out, x2, gt, seq, bm=1024, bn=1024):
    t, ko = o.shape
    k, d = w_out.shape
    assert ko + p.shape[1] == k
    bm, bn, n_row_blocks, n_col_blocks, rows = _ws_geometry(t, k, d, min(bm, seq), bn)
    per_seq = seq // bm
    row_block, chunk, col_block = _ws_maps(n_col_blocks)
    return pl.pallas_call(
        functools.partial(_mm_ws_out_kernel, n_col_blocks=n_col_blocks),
        grid=(n_col_blocks + 1, n_row_blocks),
        in_specs=[
            pl.BlockSpec((bm, ko), lambda p, i: (row_block(p, i), 0)),
            pl.BlockSpec((bm, k - ko), lambda p, i: (row_block(p, i), 0)),
            pl.BlockSpec((rows, bn), chunk),
            pl.BlockSpec((bm, bn), lambda p, i: (row_block(p, i), col_block(p, i))),
            pl.BlockSpec((1, 1, bn), lambda p, i: (row_block(p, i) // per_seq, 0, col_block(p, i))),
        ],
        out_specs=pl.BlockSpec((bm, bn), lambda p, i: (row_block(p, i), col_block(p, i))),
        out_shape=jax.ShapeDtypeStruct((t, d), F32),
        scratch_shapes=[pltpu.VMEM((2, k, bn), BF16)],
        compiler_params=_params(("arbitrary", "arbitrary")),
        name="out_proj_residual",
    )(o, p, w_out, x2, gt)


def _mm_down_kernel(a_ref, w_ref, x_ref, gt_ref, o_ref, *, nk):
    kk = pl.program_id(2)
    if nk == 1:
        o_ref[...] = x_ref[...] + gt_ref[0] * _dot(a_ref[...], w_ref[...])
        return

    @pl.when(kk == 0)
    def _():
        o_ref[...] = _dot(a_ref[...], w_ref[...])

    @pl.when(jnp.logical_and(kk > 0, kk < nk - 1))
    def _():
        o_ref[...] += _dot(a_ref[...], w_ref[...])

    @pl.when(kk == nk - 1)
    def _():
        o_ref[...] = x_ref[...] + gt_ref[0] * (o_ref[...] + _dot(a_ref[...], w_ref[...]))


def _mlp_down_residual(a, w, x2, gt, seq, bm=1024, bn=1024, bk=4096):
    t, k = a.shape
    d = w.shape[1]
    bm, bn, bk = min(bm, seq), min(bn, d), min(bk, k)
    per_seq = seq // bm
    nk = k // bk
    return pl.pallas_call(
        functools.partial(_mm_down_kernel, nk=nk),
        grid=(t // bm, d // bn, nk),
        in_specs=[
            pl.BlockSpec((bm, bk), lambda i, j, kk: (i, kk)),
            pl.BlockSpec((bk, bn), lambda i, j, kk: (kk, j)),
            pl.BlockSpec((bm, bn), lambda i, j, kk: (i, j)),
            pl.BlockSpec((1, 1, bn), lambda i, j, kk: (i // per_seq, 0, j)),
        ],
        out_specs=pl.BlockSpec((bm, bn), lambda i, j, kk: (i, j)),
        out_shape=jax.ShapeDtypeStruct((t, d), F32),
        compiler_params=_params(("parallel", "parallel", "arbitrary")),
        name="mlp_down_residual",
    )(a, w, x2, gt)


def _split3(x):
    hi = x.astype(BF16)
    r = x - hi.astype(F32)
    mid = r.astype(BF16)
    lo = (r - mid.astype(F32)).astype(BF16)
    return hi, mid, lo


def _gate_kernel(ba_ref, alog_ref, dtb_ref, col_ref, row_ref, *, hb, chunk):
    bt, lanes = ba_ref.shape
    x = ba_ref[...]
    beta = jax.nn.sigmoid(x)
    z = x + dtb_ref[...]
    softplus = jnp.maximum(z, 0.0) + jnp.log1p(jnp.exp(-jnp.abs(z)))
    g = -jnp.exp(alog_ref[...]) * softplus * LOG2_E

    ri = lax.broadcasted_iota(jnp.int32, (chunk, chunk), 0)
    ci = lax.broadcasted_iota(jnp.int32, (chunk, chunk), 1)
    tril = (ri >= ci).astype(BF16)
    quantity = lax.broadcasted_iota(jnp.int32, (chunk, lanes), 1) // hb

    for c in range(bt // chunk):
        rows = slice(c * chunk, (c + 1) * chunk)
        hi, mid, lo = _split3(g[rows])
        gc = _dot(tril, hi) + _dot(tril, mid) + _dot(tril, lo)
        out = jnp.where(quantity == 0, beta[rows], jnp.where(quantity == 1, gc, 0.0))
        col_ref[rows, :] = out
        row_ref[:, rows] = out.T


def _gates(ba, alog_l, dtb_l, hb, chunk):
    t, width = ba.shape
    n_hg = width // V7X_LANES
    bt = min(1024, t)
    return pl.pallas_call(
        functools.partial(_gate_kernel, hb=hb, chunk=chunk),
        grid=(n_hg, t // bt),
        in_specs=[
            pl.BlockSpec((bt, V7X_LANES), lambda h, i: (i, h)),
            pl.BlockSpec((1, V7X_LANES), lambda h, i: (0, h)),
            pl.BlockSpec((1, V7X_LANES), lambda h, i: (0, h)),
        ],
        out_specs=[
            pl.BlockSpec((bt, V7X_LANES), lambda h, i: (i, h)),
            pl.BlockSpec((V7X_LANES, bt), lambda h, i: (h, i)),
        ],
        out_shape=[
            jax.ShapeDtypeStruct((t, width), F32),
            jax.ShapeDtypeStruct((width, t), F32),
        ],
        compiler_params=_params(("parallel", "parallel")),
        name="delta_gates",
    )(ba, alog_l, dtb_l)


def _unit_lower_inverses(lows, blk_mask):
    c = lows[0].shape[0]
    nblk = c // SOLVE_BASE
    diags = [jnp.where(blk_mask, low, 0.0) for low in lows]
    rests = [(low - diag).astype(BF16) for low, diag in zip(lows, diags)]

    sub = lax.broadcasted_iota(jnp.int32, (SOLVE_BASE, c), 0)
    lane = lax.broadcasted_iota(jnp.int32, (SOLVE_BASE, c), 1)
    blk_start = (lane // SOLVE_BASE) * SOLVE_BASE
    compact_eye = (sub == lane - blk_start).astype(F32)

    def block_inverse(diag):
        lc = jnp.sum(diag.reshape(nblk, SOLVE_BASE, c), axis=0)
        x = compact_eye
        for j in range(SOLVE_BASE - 1):
            mult = jnp.take_along_axis(lc, blk_start + j, axis=1)
            x = x - mult * jnp.broadcast_to(x[j:j + 1, :], (SOLVE_BASE, c))
        return jnp.where(blk_mask, jnp.tile(x, (nblk, 1)), 0.0)

    invs = [block_inverse(diag) for diag in diags]
    zps = [_dot(inv.astype(BF16), rest).astype(BF16) for inv, rest in zip(invs, rests)]
    invs = [inv - _dot(zp, inv.astype(BF16)) for zp, inv in zip(zps, invs)]
    span = 2
    while span < nblk:
        zps = [_dot(zp, zp).astype(BF16) for zp in zps]
        invs = [inv + _dot(zp, inv.astype(BF16)) for zp, inv in zip(zps, invs)]
        span *= 2
    return invs


def _delta_kernel(q_ref, k_ref, v_ref, z_ref, cwq_ref, cwk_ref, cwv_ref, gcol_ref, grow_ref, ng_ref,
                  o_ref, qbuf, kbuf, vbuf, s_ref, sel_ref, *, hb, dk, kconv):
    cb = q_ref.shape[0]
    c = DELTA_CHUNK
    subs = cb // c
    assert c == dk == V7X_LANES and subs * c == cb
    pad = V7X_SUBLANES
    first = pl.program_id(2) == 0

    @pl.when(first)
    def _():
        for tail in (qbuf, kbuf, vbuf):
            tail[...] = jnp.zeros_like(tail)
        s_ref[...] = jnp.zeros_like(s_ref)
        lane_of_row = lax.broadcasted_iota(jnp.int32, (V7X_LANES, dk), 0)
        for lane in range(N_GATE_QUANTITIES * hb):
            sel_ref[lane] = (lane_of_row == lane).astype(BF16)

    def conv_silu(x_ref, tail_ref, cw_ref):
        cur = x_ref[...]
        width = cur.shape[1]
        tiles = jnp.concatenate([tail_ref[...], cur], axis=0).reshape(cb // pad + 1, pad, width)
        sublane = lax.broadcasted_iota(jnp.int32, (1, pad, width), 1)
        cw = cw_ref[...]
        y = cur * cw[kconv - 1:kconv, :]
        for s in range(1, kconv):
            rot = pltpu.roll(tiles, s, 1)
            shifted = jnp.where(sublane < s, rot[:-1], rot[1:]).reshape(cb, width)
            y = y + shifted * cw[kconv - 1 - s:kconv - s, :]
        tail_ref[...] = cur[cb - pad:cb, :]
        return _silu(y)

    qc = conv_silu(q_ref, qbuf, cwq_ref)
    kc = conv_silu(k_ref, kbuf, cwk_ref)
    vc = conv_silu(v_ref, vbuf, cwv_ref)

    ri = lax.broadcasted_iota(jnp.int32, (c, c), 0)
    ci = lax.broadcasted_iota(jnp.int32, (c, c), 1)
    causal = ri >= ci
    strict = ri > ci
    blk_mask = (ri // SOLVE_BASE) == (ci // SOLVE_BASE)
    grow = grow_ref[...]
    ng = ng_ref[...]
    heads = range(hb)

    def head_cols(h):
        return slice(h * dk, (h + 1) * dk)

    ones16 = jnp.ones((dk, dk), BF16)
    gate_pieces = _split3(gcol_ref[...])

    def gate_lanes(quantity, h, rows):
        sel = sel_ref[quantity * hb + h]
        return sum(_dot(piece[rows], sel) for piece in gate_pieces)

    def row_sum_lanes(v):
        hi = v.astype(BF16)
        lo = (v - hi.astype(F32)).astype(BF16)
        return _dot(hi, ones16) + _dot(lo, ones16)

    units = [(sub, h) for sub in range(subs) for h in heads]
    qn, kn, kb, rhs, decay, qd16, kd16, egl = [], [], [], [], [], [], [], []
    for sub, h in units:
        rows = slice(sub * c, (sub + 1) * c)
        qh, kh, vh = qc[rows, head_cols(h)], kc[rows, head_cols(h)], vc[rows, head_cols(h)]
        qn_h = qh * lax.rsqrt(row_sum_lanes(qh * qh) + EPS) * (dk ** -0.5)
        kn_h = kh * lax.rsqrt(row_sum_lanes(kh * kh) + EPS)
        beta = gate_lanes(0, h, rows)
        gc = gate_lanes(1, h, rows)
        gl = gc[c - 1:c, :]
        eg = jnp.exp2(gc)
        ekd = jnp.exp2(gl - gc)
        egl.append(jnp.exp2(gl))
        gr = grow[hb + h:hb + h + 1, rows]
        decay.append(jnp.where(causal, jnp.exp2(jnp.where(causal, gc - gr, 0.0)), 0.0))
        kb_h = kn_h * beta
        qn.append(qn_h.astype(BF16))
        kn.append(kn_h.astype(BF16))
        kb.append(kb_h.astype(BF16))
        rhs.append(jnp.concatenate([(vh * beta).astype(BF16), (kb_h * eg).astype(BF16)], axis=1))
        qd16.append((qn_h * eg).astype(BF16))
        kd16.append((kn_h * ekd).astype(BF16))

    kk = [_dot_nt(kb_u, kn_u) for kb_u, kn_u in zip(kb, kn)]
    qk16 = [(_dot_nt(qn_u, kn_u) * dec_u).astype(BF16) for qn_u, kn_u, dec_u in zip(qn, kn, decay)]
    lows = [jnp.where(strict, kk_u * dec_u, 0.0) for kk_u, dec_u in zip(kk, decay)]
    ainv = _unit_lower_inverses(lows, blk_mask)
    uw = [_dot(ainv_u.astype(BF16), rhs_u) for ainv_u, rhs_u in zip(ainv, rhs)]

    s = [s_ref[h] for h in heads]
    for sub in range(subs):
        rows = slice(sub * c, (sub + 1) * c)
        unit = [sub * hb + h for h in heads]
        s16 = [s[h].astype(BF16) for h in heads]
        ws_qs = [_dot(jnp.concatenate([uw[unit[h]][:, dk:].astype(BF16), qd16[unit[h]]], axis=0), s16[h])
                 for h in heads]
        vn16 = [(uw[unit[h]][:, :dk] - ws_qs[h][:c]).astype(BF16) for h in heads]
        o = [ws_qs[h][c:] + _dot(qk16[unit[h]], vn16[h]) for h in heads]
        s = [s[h] * egl[unit[h]] + _dot_tn(kd16[unit[h]], vn16[h]) for h in heads]
        for h in heads:
            on = o[h] * lax.rsqrt(row_sum_lanes(o[h] * o[h]) * (1.0 / dk) + EPS) * ng
            o_ref[rows, head_cols(h)] = (on * _silu(z_ref[rows, head_cols(h)])).astype(o_ref.dtype)
    for h in heads:
        s_ref[h] = s[h]


def _delta_rule(proj, conv_w, gcol, grow, dn_norm_g, bsz, seq, n_heads, dk, hb):
    t = proj.shape[0]
    c = DELTA_CHUNK * DELTA_CHUNKS_PER_STEP
    assert seq % c == 0
    nc = seq // c
    kconv = conv_w.shape[0]
    wblk = hb * dk
    n_hg = n_heads // hb
    width = n_heads * dk

    def tok(b, hg, n):
        return b * nc + n

    def proj_spec(part):
        return pl.BlockSpec((c, wblk), lambda b, hg, n: (tok(b, hg, n), part * n_hg + hg))

    def conv_spec(part):
        return pl.BlockSpec((kconv, wblk), lambda b, hg, n: (0, part * n_hg + hg))

    return pl.pallas_call(
        functools.partial(_delta_kernel, hb=hb, dk=dk, kconv=kconv),
        grid=(bsz, n_hg, nc),
        in_specs=[
            proj_spec(0), proj_spec(1), proj_spec(2), proj_spec(3),
            conv_spec(0), conv_spec(1), conv_spec(2),
            pl.BlockSpec((c, V7X_LANES), lambda b, hg, n: (tok(b, hg, n), hg)),
            pl.BlockSpec((V7X_LANES, c), lambda b, hg, n: (hg, tok(b, hg, n))),
            pl.BlockSpec((1, dk), lambda b, hg, n: (0, 0)),
        ],
        out_specs=pl.BlockSpec((c, wblk), lambda b, hg, n: (tok(b, hg, n), hg)),
        out_shape=jax.ShapeDtypeStruct((t, width), BF16),
        scratch_shapes=[
            pltpu.VMEM((V7X_SUBLANES, wblk), F32),
            pltpu.VMEM((V7X_SUBLANES, wblk), F32),
            pltpu.VMEM((V7X_SUBLANES, wblk), F32),
            pltpu.VMEM((hb, dk, dk), F32),
            pltpu.VMEM((N_GATE_QUANTITIES * hb, V7X_LANES, dk), BF16),
        ],
        compiler_params=_params(("parallel", "parallel", "arbitrary")),
        name="gated_delta_rule",
    )(proj, proj, proj, proj, conv_w, conv_w, conv_w, gcol, grow, dn_norm_g.reshape(1, dk))


def _pool_kernel(u_ref, w_ref, sc_ref, o_ref, tail_ref, *, gw, hist):
    bt = u_ref.shape[0]
    n = pl.program_id(1)

    @pl.when(n == 0)
    def _():
        tail_ref[...] = jnp.zeros_like(tail_ref)

    cur = u_ref[...]
    pos = n * bt + lax.broadcasted_iota(jnp.int32, (bt, 1), 0)
    for g, window in enumerate(POOL_WINDOWS):
        cols = slice(g * gw, (g + 1) * gw)
        cg = cur[:, cols]
        ext = jnp.concatenate([tail_ref[:, cols], cg], axis=0)
        span = 1
        while span < window:
            ext = ext + pltpu.roll(ext, span, 0)
            span *= 2
        count = jnp.minimum(pos + 1, window).astype(F32)
        pooled = ext[hist:, :] / count - cg
        p = _dot(pooled.astype(BF16), w_ref[g])
        p = p * lax.rsqrt(jnp.mean(p * p, axis=-1, keepdims=True) + EPS)
        o_ref[:, cols] = (p * sc_ref[:, cols]).astype(o_ref.dtype)
    tail_ref[...] = cur[bt - hist:bt, :]


def _multiscale_pool(pool_in, pool_w16, pool_scale, bsz, seq):
    t = pool_in.shape[0]
    ng, gw, _ = pool_w16.shape
    width = ng * gw
    hist = max(POOL_WINDOWS)
    assert all(w & (w - 1) == 0 for w in POOL_WINDOWS) and hist % V7X_SUBLANES == 0
    bt = min(512, seq)
    per_seq = seq // bt
    return pl.pallas_call(
        functools.partial(_pool_kernel, gw=gw, hist=hist),
        grid=(bsz, per_seq),
        in_specs=[
            pl.BlockSpec((bt, width), lambda b, n: (b * per_seq + n, 0)),
            pl.BlockSpec((ng, gw, gw), lambda b, n: (0, 0, 0)),
            pl.BlockSpec((1, width), lambda b, n: (0, 0)),
        ],
        out_specs=pl.BlockSpec((bt, width), lambda b, n: (b * per_seq + n, 0)),
        out_shape=jax.ShapeDtypeStruct((t, width), BF16),
        scratch_shapes=[pltpu.VMEM((hist, width), F32)],
        compiler_params=_params(("parallel", "arbitrary")),
        name="multiscale_pool",
    )(pool_in, pool_w16, pool_scale.reshape(1, width))


def _heads_per_step(n_heads):
    hb = min(n_heads, 16)
    while n_heads % hb:
        hb -= 1
    return hb


def _hybrid_layer(x2, bsz, seq, c, w_ada, b_ada, norm1_g, w_in, conv_w, a_log, dt_bias, dn_norm_g,
                  pool_w, pool_scale, w_out, norm2_g, w_ff1, w_ff2):
    d = x2.shape[1]
    n_heads = a_log.shape[0]
    dk = dn_norm_g.shape[0]
    dn_width = n_heads * dk
    pool_width = pool_scale.shape[0]
    hb = _heads_per_step(n_heads)
    n_hg = n_heads // hb
    assert N_GATE_QUANTITIES * hb <= V7X_LANES
    assert pool_width == dn_width

    mod = _modulation(c, w_ada, b_ada)
    sh1, sc1, gt1, sh2, sc2, gt2 = [m.reshape(bsz, 1, d) for m in jnp.split(mod, N_MOD, axis=-1)]

    o2 = 4 * dn_width
    o3 = o2 + n_heads
    o4 = o3 + n_heads
    w_in_t = w_in.T
    w_pool_t = w_in_t[o4:]
    w_b = w_in[:, o2:o3].reshape(d, n_hg, hb)
    w_a = w_in[:, o3:o4].reshape(d, n_hg, hb)
    lane_pad = V7X_LANES - N_GATE_QUANTITIES * hb
    w_ba = jnp.concatenate([w_b] + [w_a] * (N_GATE_QUANTITIES - 1) + [jnp.zeros((d, n_hg, lane_pad), F32)],
                           axis=2).reshape(d, n_hg * V7X_LANES)

    def lane_pack(v):
        vv = v.astype(F32).reshape(n_hg, hb)
        return jnp.concatenate([jnp.zeros((n_hg, hb), F32)] + [vv] * (N_GATE_QUANTITIES - 1)
                               + [jnp.zeros((n_hg, lane_pad), F32)], axis=1).reshape(1, n_hg * V7X_LANES)

    h1, ba = _norm_mod(x2, norm1_g, sc1, sh1, seq, w_gate=w_ba)
    proj = _matmul_ws(h1, w_in_t, o2, F32, False, "in_proj", w_transposed=True)
    pool_in = _matmul_ws(h1, w_pool_t, pool_width, F32, False, "pool_proj", w_transposed=True)
    gcol, grow = _gates(ba, lane_pack(a_log), lane_pack(dt_bias), hb, DELTA_CHUNK)

    o = _delta_rule(proj, conv_w, gcol, grow, dn_norm_g, bsz, seq, n_heads, dk, hb)
    p = _multiscale_pool(pool_in, pool_w.astype(BF16), pool_scale, bsz, seq)

    x1 = _out_proj_residual(o, p, w_out, x2, gt1, seq)

    h2 = _norm_mod(x1, norm2_g, sc2, sh2, seq)
    a, w_ff2_16 = _matmul_ws(h2, w_ff1, w_ff1.shape[1], BF16, True, "mlp_up_relu2", side=w_ff2)
    return _mlp_down_residual(a, w_ff2_16, x1, gt2, seq)


def kernel(x, c, w_ada, b_ada, norm1_g, w_in, conv_w, a_log, dt_bias, dn_norm_g, pool_w, pool_scale,
           w_out, norm2_g, w_ff1, w_ff2, final_norm_g):
    bsz, seq, d = x.shape
    x2 = x.reshape(bsz * seq, d)
    for l in range(w_ada.shape[0]):
        x2 = _hybrid_layer(x2, bsz, seq, c, w_ada[l], b_ada[l], norm1_g[l], w_in[l], conv_w[l], a_log[l],
                           dt_bias[l], dn_norm_g[l], pool_w[l], pool_scale[l], w_out[l], norm2_g[l],
                           w_ff1[l], w_ff2[l])
    return _final_norm(x2, final_norm_g).reshape(bsz, seq, d)
```
